```python
import jax, jax.numpy as jnp
from jax import lax
import numpy as np

D_MODEL = 2048
BATCH = 4
SEQ = 2048
DEPTH = 4
DEC_BATCH = 8
DEC_SEQ = 8
PAST_LEN = 16384
PAGE_SIZE = 128

HEAD_DIM = 128
N_HEADS = D_MODEL // HEAD_DIM
N_MIXERS = 2
MOBA_BLOCK = 256
MOBA_TOP_K = 3
MOBA_Q_CHUNK = 16
SB_Q_BLOCK = 128
ROPE_THETA = 10000.0
N_GROUPS = 4
EXPERTS_PER_GROUP = 8
N_EXPERTS = N_GROUPS * EXPERTS_PER_GROUP
TOP_K_EXPERTS = 2
D_EXPERT = D_MODEL // 4
LN_EPS = 1e-5
DEEPNORM_ALPHA = (2 * DEPTH) ** 0.25
DEEPNORM_BETA = (8 * DEPTH) ** -0.25

kernel_name = "moba_stickbreak_hmoe_decoder_step"


def _layernorm(x, g, b):
    xf = x.astype(jnp.float32)
    mu = jnp.mean(xf, axis=-1, keepdims=True)
    var = jnp.mean(jnp.square(xf - mu), axis=-1, keepdims=True)
    y = (xf - mu) * lax.rsqrt(var + LN_EPS)
    return (y * g.astype(jnp.float32) + b.astype(jnp.float32)).astype(x.dtype)


def _rope(x, pos):
    half = HEAD_DIM // 2
    inv_freq = ROPE_THETA ** (-jnp.arange(half, dtype=jnp.float32) / half)
    ang = pos.astype(jnp.float32)[:, None] * inv_freq[None, :]
    cos = jnp.cos(ang)[None, :, None, :]
    sin = jnp.sin(ang)[None, :, None, :]
    xf = x.astype(jnp.float32)
    x1, x2 = xf[..., :half], xf[..., half:]
    return jnp.concatenate([x1 * cos - x2 * sin, x2 * cos + x1 * sin], axis=-1).astype(x.dtype)


def _query_sweep(fn, block, q_arrays, pos):
    t = pos.shape[0]
    if t <= block or t % block:
        return fn(*q_arrays, pos)
    n = t // block

    def split(a):
        a = a.reshape(a.shape[:2] + (n, block) + a.shape[3:])
        return jnp.moveaxis(a, 2, 0)

    out = lax.map(lambda args: fn(*args[0], args[1]),
                  (tuple(split(a) for a in q_arrays), pos.reshape(n, block)))
    out = jnp.moveaxis(out, 0, 2)
    return out.reshape(out.shape[:2] + (t,) + out.shape[4:])


def _moba(q, pos, k, v):
    b, h, t, d = q.shape
    L = k.shape[2]
    nb = -(-L // MOBA_BLOCK)
    pad = nb * MOBA_BLOCK - L
    kb = jnp.pad(k, ((0, 0), (0, 0), (0, pad), (0, 0))).reshape(b, h, nb, MOBA_BLOCK, d)
    vb = jnp.pad(v, ((0, 0), (0, 0), (0, pad), (0, 0))).reshape(b, h, nb, MOBA_BLOCK, d)
    means = jnp.mean(kb.astype(jnp.float32), axis=3)
    cur = pos // MOBA_BLOCK
    gate = jnp.einsum('bhtd,bhnd->bhtn', q.astype(jnp.float32), means)
    gate = jnp.where(jnp.arange(nb)[None, :] < cur[:, None], gate, -jnp.inf)
    _, top = lax.top_k(gate, min(MOBA_TOP_K, nb))
    own = jnp.broadcast_to(cur[None, None, :, None], (b, h, t, 1))
    sel = jnp.concatenate([top.astype(jnp.int32), own.astype(jnp.int32)], axis=-1)
    ok = jnp.concatenate([top < cur[None, None, :, None], jnp.ones((b, h, t, 1), bool)], axis=-1)
    scale = HEAD_DIM ** -0.5
    gather = jax.vmap(jax.vmap(lambda blocks, idx: blocks[idx]))

    def chunk(q_c, sel_c, ok_c, t_c):
        k_sel = gather(kb, sel_c)
        v_sel = gather(vb, sel_c)
        kpos = sel_c[..., None] * MOBA_BLOCK + jnp.arange(MOBA_BLOCK)
        mask = ok_c[..., None] & (kpos <= t_c[None, None, :, None, None])
        s = jnp.einsum('bhtd,bhtjkd->bhtjk', q_c, k_sel).astype(jnp.float32) * scale
        p = jax.nn.softmax(jnp.where(mask, s, -jnp.inf), axis=(-2, -1))
        return jnp.einsum('bhtjk,bhtjkd->bhtd', p.astype(v_sel.dtype), v_sel)

    return _query_sweep(chunk, MOBA_Q_CHUNK, (q, sel, ok), pos)


def _stick_breaking(q, pos, k, v):
    kpos = jnp.arange(k.shape[2])
    scale = HEAD_DIM ** -0.5

    def block(q_c, t_c):
        z = jnp.einsum('bhtd,bhsd->bhts', q_c, k).astype(jnp.float32) * scale
        causal = kpos[None, :] < t_c[:, None]
        log_stay = jnp.where(causal, jax.nn.log_sigmoid(-z), 0.0)
        after = lax.cumsum(log_stay, axis=3, reverse=True) - log_stay
        log_a = jnp.where(causal, jax.nn.log_sigmoid(z) + after, -jnp.inf)
        return jnp.einsum('bhts,bhsd->bhtd', jnp.exp(log_a).astype(v.dtype), v)

    return _query_sweep(block, SB_Q_BLOCK, (q,), pos)


def _mixer(kind, x, pos, k_past, v_past, w_qkv_l, w_o_l):
    b, t, _ = x.shape
    qkv = (x @ w_qkv_l).reshape(b, t, 3, N_HEADS, HEAD_DIM)
    q, k, v = qkv[:, :, 0], qkv[:, :, 1], qkv[:, :, 2]
    if kind == 0:
        q, k = _rope(q, pos), _rope(k, pos)
    k_all = k if k_past is None else jnp.concatenate([k_past.astype(k.dtype), k], axis=1)
    v_all = v if v_past is None else jnp.concatenate([v_past.astype(v.dtype), v], axis=1)
    qh = q.transpose(0, 2, 1, 3)
    kh = k_all.transpose(0, 2, 1, 3)
    vh = v_all.transpose(0, 2, 1, 3)
    o = _moba(qh, pos, kh, vh) if kind == 0 else _stick_breaking(qh, pos, kh, vh)
    o = o.transpose(0, 2, 1, 3).reshape(b, t, D_MODEL)
    return o @ w_o_l, k, v


def _hier_moe(x, w_group, b_group, w_route, b_route, w_gate, w_up, w_down):
    shp = x.shape
    xt = x.reshape(-1, D_MODEL)
    g_prob = jax.nn.softmax((xt @ w_group).astype(jnp.float32) + b_group.astype(jnp.float32), axis=-1)
    g_p, g_idx = lax.top_k(g_prob, 1)
    e_logits = ((xt @ w_route).astype(jnp.float32) + b_route.astype(jnp.float32)).reshape(-1, N_GROUPS, EXPERTS_PER_GROUP)
    e_logits = jnp.take_along_axis(e_logits, g_idx[:, :, None], axis=1)[:, 0]
    e_val, e_idx = lax.top_k(e_logits, TOP_K_EXPERTS)
    e_w = jax.nn.softmax(e_val, axis=-1) * g_p
    flat = g_idx * EXPERTS_PER_GROUP + e_idx
    combine = jnp.sum(jax.nn.one_hot(flat, N_EXPERTS, dtype=jnp.float32) * e_w[..., None], axis=1)
    h = jax.nn.silu(jnp.einsum('nd,edf->nef', xt, w_gate)) * jnp.einsum('nd,edf->nef', xt, w_up)
    h = h * combine[:, :, None].astype(h.dtype)
    y = jnp.einsum('nef,efd->nd', h, w_down)
    return y.reshape(shp)


def setup_inputs(seed: int = 0) -> dict:
    key = jax.random.key(seed)
    ks = jax.random.split(key, 20)
    n_pages = PAST_LEN // PAGE_SIZE
    in_use = DEC_BATCH * n_pages
    n_pool = in_use + max(1, in_use // 4)
    f32 = jnp.float32
    s_d = D_MODEL ** -0.5
    return {
        "x_prompt": jax.random.normal(ks[0], (BATCH, SEQ, D_MODEL), f32),
        "x_sample": jax.random.normal(ks[1], (DEC_BATCH, DEC_SEQ, D_MODEL), f32),
        "cache_k": jax.random.normal(ks[2], (DEPTH, n_pool, PAGE_SIZE, N_HEADS, HEAD_DIM), f32),
        "cache_v": jax.random.normal(ks[3], (DEPTH, n_pool, PAGE_SIZE, N_HEADS, HEAD_DIM), f32),
        "page_table": jax.random.permutation(ks[4], n_pool)[:in_use].reshape(DEC_BATCH, n_pages).astype(jnp.int32),
        "w_qkv": jax.random.normal(ks[5], (DEPTH, D_MODEL, 3 * D_MODEL), f32) * s_d,
        "w_o": jax.random.normal(ks[6], (DEPTH, D_MODEL, D_MODEL), f32) * (s_d * DEEPNORM_BETA),
        "ln_mix_g": 1.0 + 0.02 * jax.random.normal(ks[7], (DEPTH, D_MODEL), f32),
        "ln_mix_b": 0.02 * jax.random.normal(ks[8], (DEPTH, D_MODEL), f32),
        "w_group": jax.random.normal(ks[9], (DEPTH, D_MODEL, N_GROUPS), f32) * s_d,
        "b_group": 0.01 * jax.random.normal(ks[10], (DEPTH, N_GROUPS), f32),
        "w_route": jax.random.normal(ks[11], (DEPTH, D_MODEL, N_EXPERTS), f32) * s_d,
        "b_route": 0.01 * jax.random.normal(ks[12], (DEPTH, N_EXPERTS), f32),
        "w_gate": jax.random.normal(ks[13], (DEPTH, N_EXPERTS, D_MODEL, D_EXPERT), f32) * s_d,
        "w_up": jax.random.normal(ks[14], (DEPTH, N_EXPERTS, D_MODEL, D_EXPERT), f32) * s_d,
        "w_down": jax.random.normal(ks[15], (DEPTH, N_EXPERTS, D_EXPERT, D_MODEL), f32) * (D_EXPERT ** -0.5 * DEEPNORM_BETA),
        "ln_ffn_g": 1.0 + 0.02 * jax.random.normal(ks[16], (DEPTH, D_MODEL), f32),
        "ln_ffn_b": 0.02 * jax.random.normal(ks[17], (DEPTH, D_MODEL), f32),
    }


def reference(x_prompt, x_sample, cache_k, cache_v, page_table, w_qkv, w_o, ln_mix_g, ln_mix_b,
              w_group, b_group, w_route, b_route, w_gate, w_up, w_down, ln_ffn_g, ln_ffn_b):
    dec_b, t_s, _ = x_sample.shape
    t_p = x_prompt.shape[1]
    past_len = page_table.shape[1] * cache_k.shape[2]
    pos_p = jnp.arange(t_p, dtype=jnp.int32)
    pos_s = past_len + jnp.arange(t_s, dtype=jnp.int32)
    x_p, x_s = x_prompt, x_sample
    kp, vp, ks, vs = [], [], [], []
    for l in range(DEPTH):
        kind = l % N_MIXERS
        mix_p, k_new_p, v_new_p = _mixer(kind, x_p, pos_p, None, None, w_qkv[l], w_o[l])
        k_past = cache_k[l, page_table].reshape(dec_b, past_len, N_HEADS, HEAD_DIM)
        v_past = cache_v[l, page_table].reshape(dec_b, past_len, N_HEADS, HEAD_DIM)
        mix_s, k_new_s, v_new_s = _mixer(kind, x_s, pos_s, k_past, v_past, w_qkv[l], w_o[l])
        x_p = _layernorm(DEEPNORM_ALPHA * x_p + mix_p, ln_mix_g[l], ln_mix_b[l])
        x_s = _layernorm(DEEPNORM_ALPHA * x_s + mix_s, ln_mix_g[l], ln_mix_b[l])
        moe_w = (w_group[l], b_group[l], w_route[l], b_route[l], w_gate[l], w_up[l], w_down[l])
        x_p = _layernorm(DEEPNORM_ALPHA * x_p + _hier_moe(x_p, *moe_w), ln_ffn_g[l], ln_ffn_b[l])
        x_s = _layernorm(DEEPNORM_ALPHA * x_s + _hier_moe(x_s, *moe_w), ln_ffn_g[l], ln_ffn_b[l])
        kp.append(k_new_p)
        vp.append(v_new_p)
        ks.append(k_new_s)
        vs.append(v_new_s)
    return (x_p, x_s, jnp.stack(kp), jnp.stack(vp), jnp.stack(ks), jnp.stack(vs))
```

```python
import functools

import jax
import jax.numpy as jnp
from jax import lax
from jax.experimental import pallas as pl
from jax.experimental.pallas import tpu as pltpu

HEAD_DIM = 128
MOBA_BLOCK = 256
MOBA_TOP_K = 3
ROPE_THETA = 10000.0
LN_EPS = 1e-5
TOP_K_EXPERTS = 2
LANES = 128
ROW_TILE = 256
NEG = -1e30
VMEM_LIMIT = 48 * 1024 * 1024

F32 = jnp.float32
BF16 = jnp.bfloat16
NT_DIMS = (((1,), (1,)), ((), ()))
NN_DIMS = (((1,), (0,)), ((), ()))


def _split_bf16(a):
    hi = a.astype(BF16)
    lo = (a - hi.astype(F32)).astype(BF16)
    return hi, lo


def _dot3(a, b, dims):
    ah, al = _split_bf16(a)
    bh, bl = _split_bf16(b)
    d = lambda x, y: lax.dot_general(x, y, dims, preferred_element_type=F32)
    return d(ah, bh) + d(ah, bl) + d(al, bh)


def _dot(a, b, dims=NN_DIMS):
    return lax.dot_general(a, b, dims, preferred_element_type=F32)


def _pick_tile(n, target, quantum):
    best = quantum
    t = quantum
    while t <= target:
        if n % t == 0:
            best = t
        t += quantum
    assert n % best == 0, (n, quantum)
    return best


def _layernorm(h, g, b):
    mu = jnp.mean(h, axis=-1, keepdims=True)
    d = h - mu
    var = jnp.mean(d * d, axis=-1, keepdims=True)
    return d * lax.rsqrt(var + LN_EPS) * g + b


def _log_sigmoid_pair(z):
    sp = jnp.log1p(jnp.exp(-jnp.abs(z)))
    pos = jnp.minimum(z, 0.0) - sp
    return pos, pos - z


def _topk_bias(g, n_blocks, n_valid):
    blk = lax.broadcasted_iota(jnp.int32, g.shape, 0)
    rank = jnp.zeros(g.shape, jnp.int32)
    for n in range(n_blocks):
        gn = g[n:n + 1, :]
        beats = (gn > g) | ((gn == g) & (blk > n))
        inc = jnp.where(beats, 1, 0)
        if n_valid is not None:
            inc = inc * (n < n_valid).astype(jnp.int32)
        rank = rank + inc
    sel = rank < MOBA_TOP_K
    if n_valid is not None:
        sel = sel & (blk < n_valid)
    return jnp.where(sel, 0.0, NEG)


def _qkv_kernel(x_ref, w_ref, cos_ref, sin_ref, o_ref, *, rope_tiles):
    acc = _dot(x_ref[...], w_ref[...])
    if rope_tiles == 0:
        o_ref[...] = acc
        return
    j = pl.program_id(0)

    @pl.when(j < rope_tiles)
    def _():
        cos = cos_ref[...]
        sin = sin_ref[...]
        for c in range(acc.shape[1] // HEAD_DIM):
            blk = acc[:, c * HEAD_DIM:(c + 1) * HEAD_DIM]
            rot = pltpu.roll(blk, HEAD_DIM // 2, axis=1)
            o_ref[:, c * HEAD_DIM:(c + 1) * HEAD_DIM] = blk * cos + rot * sin

    @pl.when(j >= rope_tiles)
    def _():
        o_ref[...] = acc


def _qkv_proj(xb, w_bf, cos, sin, layer, rope):
    nt, d = xb.shape
    tm = _pick_tile(nt, 768, ROW_TILE)
    tn = min(512, d)
    ncol = d // tn
    return pl.pallas_call(
        functools.partial(_qkv_kernel, rope_tiles=2 * ncol if rope else 0),
        grid=(3 * ncol, nt // tm),
        in_specs=[
            pl.BlockSpec((tm, d), lambda j, i: (i, 0)),
            pl.BlockSpec((None, d, tn), lambda j, i: (layer, 0, j)),
            pl.BlockSpec((tm, HEAD_DIM), lambda j, i: (i, 0)),
            pl.BlockSpec((tm, HEAD_DIM), lambda j, i: (i, 0)),
        ],
        out_specs=pl.BlockSpec((None, tm, tn), lambda j, i: (j // ncol, i, j % ncol)),
        out_shape=jax.ShapeDtypeStruct((3, nt, d), F32),
        compiler_params=pltpu.CompilerParams(
            dimension_semantics=("parallel", "parallel"), vmem_limit_bytes=VMEM_LIMIT),
        name="qkv_proj",
    )(xb, w_bf, cos, sin)


def _prompt_attn_kernel(q_ref, k_ref, v_ref, o_ref, kb_ref, vt_ref, *scratch, kind, nb):
    blk = MOBA_BLOCK
    i = pl.program_id(2)
    scale = HEAD_DIM ** -0.5

    @pl.when(i == 0)
    def _():
        if kind == 0:
            scratch[0][...] = jnp.zeros(scratch[0].shape, F32)
        for j in range(nb):
            kj = k_ref[j * blk:(j + 1) * blk, :]
            kb_ref[j] = kj.astype(BF16)
            vt_ref[j] = v_ref[j * blk:(j + 1) * blk, :].T.astype(BF16)
            if kind == 0:
                scratch[0][j:j + 1, :] = jnp.sum(kj, axis=0, keepdims=True) * (1.0 / blk)

    q = q_ref[...]
    qs = (q * scale).astype(BF16)
    key_i = lax.broadcasted_iota(jnp.int32, (blk, blk), 0)
    qry_i = lax.broadcasted_iota(jnp.int32, (blk, blk), 1)

    def scores(j):
        return _dot(kb_ref[j], qs, NT_DIMS)

    if kind == 0:
        means_ref, bias_ref = scratch
        gate = _dot3(means_ref[...], q, NT_DIMS)
        bias_ref[...] = _topk_bias(gate, nb, i)

        s = jnp.where(key_i <= qry_i, scores(i), NEG)
        m = jnp.max(s, axis=0, keepdims=True)
        p = jnp.exp(s - m)
        l = jnp.sum(p, axis=0, keepdims=True)
        acc = _dot(vt_ref[i], p.astype(BF16))

        def body(j, carry):
            m, l, acc = carry
            s = scores(j) + bias_ref[pl.ds(j, 1), :]
            m_new = jnp.maximum(m, jnp.max(s, axis=0, keepdims=True))
            a = jnp.exp(m - m_new)
            p = jnp.exp(s - m_new)
            l = a * l + jnp.sum(p, axis=0, keepdims=True)
            acc = a * acc + _dot(vt_ref[j], p.astype(BF16))
            return m_new, l, acc

        m, l, acc = lax.fori_loop(0, i, body, (m, l, acc))
        o_ref[...] = (acc / l).T.astype(o_ref.dtype)
    else:
        tri = jnp.where(qry_i > key_i, 1.0, 0.0).astype(BF16)

        def sb_block(j, c, acc, mask):
            z = scores(j)
            ls_pos, stay = _log_sigmoid_pair(z)
            if mask is not None:
                stay = jnp.where(mask, stay, 0.0)
            hi, lo = _split_bf16(stay)
            after = _dot(tri, hi) + _dot(tri, lo)
            a = jnp.exp(ls_pos + after + c)
            if mask is not None:
                a = jnp.where(mask, a, 0.0)
            acc = acc + _dot(vt_ref[j], a.astype(BF16))
            c = c + jnp.sum(stay, axis=0, keepdims=True)
            return c, acc

        c0 = jnp.zeros((1, blk), F32)
        acc0 = jnp.zeros((HEAD_DIM, blk), F32)
        c, acc = sb_block(i, c0, acc0, key_i < qry_i)

        def body(jj, carry):
            return sb_block(i - 1 - jj, carry[0], carry[1], None)

        c, acc = lax.fori_loop(0, i, body, (c, acc))
        o_ref[...] = acc.T.astype(o_ref.dtype)


def _prompt_attn(qkv, kind, batch, seq):
    d = qkv.shape[2]
    heads = d // HEAD_DIM
    blk = MOBA_BLOCK
    assert seq % blk == 0
    nb = seq // blk
    scratch = [pltpu.VMEM((nb, blk, HEAD_DIM), BF16), pltpu.VMEM((nb, HEAD_DIM, blk), BF16)]
    if kind == 0:
        nbp = -(-nb // 8) * 8
        scratch += [pltpu.VMEM((nbp, HEAD_DIM), F32), pltpu.VMEM((nbp, blk), F32)]
    return pl.pallas_call(
        functools.partial(_prompt_attn_kernel, kind=kind, nb=nb),
        grid=(batch, heads, nb),
        in_specs=[
            pl.BlockSpec((None, blk, HEAD_DIM), lambda b, h, i: (0, b * nb + i, h)),
            pl.BlockSpec((None, seq, HEAD_DIM), lambda b, h, i: (1, b, h)),
            pl.BlockSpec((None, seq, HEAD_DIM), lambda b, h, i: (2, b, h)),
        ],
        out_specs=pl.BlockSpec((blk, HEAD_DIM), lambda b, h, i: (b * nb + i, h)),
        out_shape=jax.ShapeDtypeStruct((batch * seq, d), BF16),
        scratch_shapes=scratch,
        compiler_params=pltpu.CompilerParams(
            dimension_semantics=("parallel", "parallel", "arbitrary"), vmem_limit_bytes=VMEM_LIMIT),
        name="prompt_attn_moba" if kind == 0 else "prompt_attn_sb",
    )(qkv, qkv, qkv)


def _cache_means_kernel(pt_ref, *refs, n_in, page, heads):
    o_ref = refs[n_in]
    tot = jnp.zeros((heads, HEAD_DIM), F32)
    for u in range(n_in):
        tot = tot + jnp.sum(refs[u][...].reshape(page, heads, HEAD_DIM), axis=0)
    o_ref[...] = tot * (1.0 / (page * n_in))


def _cache_means(cache2, page_table, layer, page, heads):
    db, n_pages = page_table.shape
    ppb = MOBA_BLOCK // page
    nbp = n_pages // ppb
    rows = page * heads

    def page_spec(u):
        return pl.BlockSpec((None, None, rows, HEAD_DIM),
                            lambda b, n, pt: (layer, pt[b, n * ppb + u], 0, 0))

    return pl.pallas_call(
        functools.partial(_cache_means_kernel, n_in=ppb, page=page, heads=heads),
        grid_spec=pltpu.PrefetchScalarGridSpec(
            num_scalar_prefetch=1,
            grid=(db, nbp),
            in_specs=[page_spec(u) for u in range(ppb)],
            out_specs=pl.BlockSpec((None, None, heads, HEAD_DIM), lambda b, n, pt: (b, n, 0, 0)),
        ),
        out_shape=jax.ShapeDtypeStruct((db, nbp, heads, HEAD_DIM), F32),
        compiler_params=pltpu.CompilerParams(
            dimension_semantics=("parallel", "parallel"), vmem_limit_bytes=VMEM_LIMIT),
        name="cache_means",
    )(page_table, *([cache2] * ppb))


def _block_diag_queries(q, heads, ts):
    rep = jnp.concatenate([q] * heads, axis=0) if heads > 1 else q
    r = lax.broadcasted_iota(jnp.int32, rep.shape, 0) // ts
    c = lax.broadcasted_iota(jnp.int32, rep.shape, 1) // HEAD_DIM
    return jnp.where(r == c, rep, 0.0)


def _moba_select_kernel(q_ref, means_ref, bias_ref, *, heads, ts, nbp):
    qbd = _block_diag_queries(q_ref[...], heads, ts)
    gate = _dot3(means_ref[...], qbd, NT_DIMS)
    bias_ref[...] = _topk_bias(gate, nbp, None)


def _moba_select(q_s, means, heads):
    db, ts, d = q_s.shape
    nbp = means.shape[1]
    return pl.pallas_call(
        functools.partial(_moba_select_kernel, heads=heads, ts=ts, nbp=nbp),
        grid=(db,),
        in_specs=[pl.BlockSpec((None, ts, d), lambda b: (b, 0, 0)),
                  pl.BlockSpec((None, nbp, d), lambda b: (b, 0, 0))],
        out_specs=pl.BlockSpec((None, nbp, heads * ts), lambda b: (b, 0, 0)),
        out_shape=jax.ShapeDtypeStruct((db, nbp, heads * ts), F32),
        compiler_params=pltpu.CompilerParams(
            dimension_semantics=("parallel",), vmem_limit_bytes=VMEM_LIMIT),
        name="moba_select",
    )(q_s, means)


def _sample_attn_kernel(pt_ref, *refs, kind, heads, ts, page, n_pages, ppb):
    if kind == 0:
        q_ref, knew_ref, vnew_ref, bias_ref, k_ref, v_ref, o_ref, qbd_ref, acc_ref, m_ref, l_ref = refs
    else:
        q_ref, knew_ref, vnew_ref, k_ref, v_ref, o_ref, qbd_ref, acc_ref, c_ref = refs
    s_id = pl.program_id(1)
    ht = heads * ts
    scale = HEAD_DIM ** -0.5
    kpos = lax.broadcasted_iota(jnp.int32, (page, ht), 0)
    qry = lax.broadcasted_iota(jnp.int32, (page, ht), 1) % ts
    row_head_p = lax.broadcasted_iota(jnp.int32, (ht, page), 0) // ts
    row_head_d = lax.broadcasted_iota(jnp.int32, (ht, HEAD_DIM), 0) // ts

    def head_rows(ref, h):
        return ref[pl.ds(h, page, stride=heads), :]

    def page_scores(kr):
        z = jnp.zeros((page, ht), F32)
        for h in range(heads):
            z = z + _dot(head_rows(kr, h).astype(BF16), qbd_ref[h], NT_DIMS)
        return z

    def weighted_values(w, vr):
        wt = w.T
        out = jnp.zeros((ht, HEAD_DIM), F32)
        for h in range(heads):
            wh = jnp.where(row_head_p == h, wt, 0.0).astype(BF16)
            out = out + _dot(wh, head_rows(vr, h).astype(BF16))
        return out

    def lanes_to_rows(v):
        return jnp.broadcast_to(v, (HEAD_DIM, ht)).T

    if kind == 0:
        def process(kr, vr, mask, bias):
            z = page_scores(kr)
            if mask is not None:
                z = jnp.where(mask, z, NEG)
            if bias is not None:
                z = z + bias
            m_old = m_ref[...]
            m_new = jnp.maximum(m_old, jnp.max(z, axis=0, keepdims=True))
            a = jnp.exp(m_old - m_new)
            p = jnp.exp(z - m_new)
            l_ref[...] = a * l_ref[...] + jnp.sum(p, axis=0, keepdims=True)
            m_ref[...] = m_new
            acc_ref[...] = acc_ref[...] * lanes_to_rows(a) + weighted_values(p, vr)
    else:
        tri = jnp.where(lax.broadcasted_iota(jnp.int32, (page, page), 1)
                        > lax.broadcasted_iota(jnp.int32, (page, page), 0), 1.0, 0.0).astype(BF16)

        def process(kr, vr, mask, bias):
            z = page_scores(kr)
            ls_pos, stay = _log_sigmoid_pair(z)
            if mask is not None:
                stay = jnp.where(mask, stay, 0.0)
            hi, lo = _split_bf16(stay)
            after = _dot(tri, hi) + _dot(tri, lo)
            a = jnp.exp(ls_pos + after + c_ref[...])
            if mask is not None:
                a = jnp.where(mask, a, 0.0)
            acc_ref[...] = acc_ref[...] + weighted_values(a, vr)
            c_ref[...] = c_ref[...] + jnp.sum(stay, axis=0, keepdims=True)

    @pl.when(s_id == 0)
    def _():
        qbd = _block_diag_queries(q_ref[...], heads, ts) * scale
        for h in range(heads):
            qbd_ref[h] = qbd[:, h * HEAD_DIM:(h + 1) * HEAD_DIM].astype(BF16)
        acc_ref[...] = jnp.zeros((ht, HEAD_DIM), F32)
        if kind == 0:
            m_ref[...] = jnp.full((1, ht), NEG, F32)
            l_ref[...] = jnp.zeros((1, ht), F32)
            process(knew_ref, vnew_ref, kpos <= qry, None)
        else:
            c_ref[...] = jnp.zeros((1, ht), F32)
            process(knew_ref, vnew_ref, kpos < qry, None)

    if kind == 0:
        blk_id = (n_pages - 1 - s_id) // ppb
        process(k_ref, v_ref, None, bias_ref[pl.ds(blk_id, 1), :])
    else:
        process(k_ref, v_ref, None, None)

    @pl.when(s_id == n_pages - 1)
    def _():
        if kind == 0:
            o_ref[...] = acc_ref[...] * lanes_to_rows(1.0 / l_ref[...])
        else:
            o_ref[...] = acc_ref[...]


def _sample_attn(q_s, knew, vnew, bias, cache_k2, cache_v2, page_table, layer, kind, heads, page):
    db, ts, d = q_s.shape
    n_pages = page_table.shape[1]
    ppb = MOBA_BLOCK // page
    ht = heads * ts
    rows = page * heads
    per_b = lambda shape: pl.BlockSpec((None,) + shape, lambda b, s, pt: (b, 0, 0))
    page_spec = pl.BlockSpec((None, None, rows, HEAD_DIM),
                             lambda b, s, pt: (layer, pt[b, n_pages - 1 - s], 0, 0))
    in_specs = [per_b((ts, d)), per_b((rows, HEAD_DIM)), per_b((rows, HEAD_DIM))]
    args = [q_s, knew, vnew]
    scratch = [pltpu.VMEM((heads, ht, HEAD_DIM), BF16), pltpu.VMEM((ht, HEAD_DIM), F32)]
    if kind == 0:
        in_specs.append(per_b(bias.shape[1:]))
        args.append(bias)
        scratch += [pltpu.VMEM((1, ht), F32), pltpu.VMEM((1, ht), F32)]
    else:
        scratch += [pltpu.VMEM((1, ht), F32)]
    in_specs += [page_spec, page_spec]
    args += [cache_k2, cache_v2]
    return pl.pallas_call(
        functools.partial(_sample_attn_kernel, kind=kind, heads=heads, ts=ts, page=page,
                          n_pages=n_pages, ppb=ppb),
        grid_spec=pltpu.PrefetchScalarGridSpec(
            num_scalar_prefetch=1,
            grid=(db, n_pages),
            in_specs=in_specs,
            out_specs=pl.BlockSpec((None, ht, HEAD_DIM), lambda b, s, pt: (b, 0, 0)),
            scratch_shapes=scratch,
        ),
        out_shape=jax.ShapeDtypeStruct((db, ht, HEAD_DIM), F32),
        compiler_params=pltpu.CompilerParams(
            dimension_semantics=("parallel", "arbitrary"), vmem_limit_bytes=VMEM_LIMIT),
        name="sample_attn_moba" if kind == 0 else "sample_attn_sb",
    )(page_table, *args)


def _proj_ln_kernel(o_ref, w_ref, x_ref, g_ref, b_ref, y_ref, yb_ref, *, alpha):
    h = alpha * x_ref[...] + _dot(o_ref[...], w_ref[...])
    y = _layernorm(h, g_ref[...], b_ref[...])
    y_ref[...] = y
    yb_ref[...] = y.astype(BF16)


def _proj_ln(o, w_bf, x, g, b, layer, alpha):
    nt, d = x.shape
    tm = ROW_TILE
    return pl.pallas_call(
        functools.partial(_proj_ln_kernel, alpha=alpha),
        grid=(nt // tm,),
        in_specs=[
            pl.BlockSpec((tm, d), lambda i: (i, 0)),
            pl.BlockSpec((None, d, d), lambda i: (layer, 0, 0)),
            pl.BlockSpec((tm, d), lambda i: (i, 0)),
            pl.BlockSpec((None, 1, d), lambda i: (layer, 0, 0)),
            pl.BlockSpec((None, 1, d), lambda i: (layer, 0, 0)),
        ],
        out_specs=[pl.BlockSpec((tm, d), lambda i: (i, 0)), pl.BlockSpec((tm, d), lambda i: (i, 0))],
        out_shape=[jax.ShapeDtypeStruct((nt, d), F32), jax.ShapeDtypeStruct((nt, d), BF16)],
        compiler_params=pltpu.CompilerParams(
            dimension_semantics=("parallel",), vmem_limit_bytes=VMEM_LIMIT),
        name="proj_ln",
    )(o, w_bf, x, g, b)


def _router_kernel(x_ref, w_ref, b_ref, o_ref, *, ng, epg):
    logits = _dot3(x_ref[...], w_ref[...], NN_DIMS) + b_ref[...]
    lane = lax.broadcasted_iota(jnp.int32, logits.shape, 1).astype(F32)
    ninf = -jnp.inf

    def top(v):
        m = jnp.max(v, axis=1, keepdims=True)
        idx = jnp.min(jnp.where(v == m, lane, float(LANES)), axis=1, keepdims=True)
        return m, idx

    gl = jnp.where(lane < ng, logits, ninf)
    gmax, gidx = top(gl)
    g_p = 1.0 / jnp.sum(jnp.exp(gl - gmax), axis=1, keepdims=True)
    lo = ng + gidx * epg
    el = jnp.where((lane >= lo) & (lane < lo + epg), logits, ninf)
    m1, i1 = top(el)
    m2, i2 = top(jnp.where(lane == i1, ninf, el))
    t = jnp.exp(m2 - m1)
    w0 = g_p / (1.0 + t)
    w1 = g_p * t / (1.0 + t)
    o_ref[...] = jnp.where(lane == 0, i1 - ng,
                           jnp.where(lane == 1, i2 - ng,
                                     jnp.where(lane == 2, w0, jnp.where(lane == 3, w1, 0.0))))


def _router(x, w, b, ng, epg):
    nt, d = x.shape
    tm = ROW_TILE
    return pl.pallas_call(
        functools.partial(_router_kernel, ng=ng, epg=epg),
        grid=(nt // tm,),
        in_specs=[pl.BlockSpec((tm, d), lambda i: (i, 0)),
                  pl.BlockSpec((d, LANES), lambda i: (0, 0)),
                  pl.BlockSpec((1, LANES), lambda i: (0, 0))],
        out_specs=pl.BlockSpec((tm, LANES), lambda i: (i, 0)),
        out_shape=jax.ShapeDtypeStruct((nt, LANES), F32),
        compiler_params=pltpu.CompilerParams(
            dimension_semantics=("parallel",), vmem_limit_bytes=VMEM_LIMIT),
        name="router",
    )(x, w, b)


def _route_plan(e_idx, ne, tm):
    n2 = e_idx.size
    flat = e_idx.reshape(-1)
    onehot = (flat[:, None] == jnp.arange(ne, dtype=jnp.int32)[None, :]).astype(jnp.int32)
    csum = jnp.cumsum(onehot, axis=0)
    counts = csum[-1]
    pos_in_e = jnp.sum((csum - onehot) * onehot, axis=1)
    tiles_e = (counts + tm - 1) // tm
    tile_end = jnp.cumsum(tiles_e)
    tile_start = tile_end - tiles_e
    dest = jnp.sum(onehot * tile_start[None, :], axis=1) * tm + pos_in_e
    n_tiles = -(-n2 // tm) + ne
    gidx = jnp.arange(n_tiles, dtype=jnp.int32)
    active = gidx < tile_end[-1]
    te = jnp.sum((gidx[:, None] >= tile_end[None, :]).astype(jnp.int32), axis=1)
    last_e = jnp.max(jnp.where(counts > 0, jnp.arange(ne, dtype=jnp.int32), 0))
    te = jnp.where(active, te, last_e).astype(jnp.int32)
    tok = jnp.zeros((n_tiles * tm,), jnp.int32).at[dest].set(jnp.arange(n2, dtype=jnp.int32) // 2)
    return dest.astype(jnp.int32), tok, te, active.astype(jnp.int32), n_tiles


def _gather_rows(idx_ref, base, n, src_hbm, dst, sem):
    def start(r, carry):
        pltpu.make_async_copy(src_hbm.at[pl.ds(idx_ref[base + r], 1)], dst.at[pl.ds(r, 1)], sem).start()
        return carry

    def wait(r, carry):
        pltpu.make_async_copy(src_hbm.at[pl.ds(0, 1)], dst.at[pl.ds(r, 1)], sem).wait()
        return carry

    lax.fori_loop(0, n, start, 0)
    lax.fori_loop(0, n, wait, 0)


def _expert_kernel(te_ref, act_ref, tok_ref, x_hbm, wg_ref, wu_ref, wd_ref, y_ref,
                   xbuf, wgb, wub, wdb, sem, *, tm):
    g = pl.program_id(0)
    prev = te_ref[jnp.maximum(g - 1, 0)]

    @pl.when((g == 0) | (te_ref[g] != prev))
    def _():
        wgb[...] = wg_ref[...].astype(BF16)
        wub[...] = wu_ref[...].astype(BF16)
        wdb[...] = wd_ref[...].astype(BF16)

    @pl.when(act_ref[g] == 1)
    def _():
        _gather_rows(tok_ref, g * tm, tm, x_hbm, xbuf, sem)
        xb = xbuf[...].astype(BF16)
        gate = _dot(xb, wgb[...])
        up = _dot(xb, wub[...])
        h = gate * jax.nn.sigmoid(gate) * up
        y_ref[...] = _dot(h.astype(BF16), wdb[...])

    @pl.when(act_ref[g] == 0)
    def _():
        y_ref[...] = jnp.zeros(y_ref.shape, F32)


def _expert_mlp(x, tok, te, act, n_tiles, w_gate, w_up, w_down, layer, tm):
    nt, d = x.shape
    f = w_gate.shape[-1]
    return pl.pallas_call(
        functools.partial(_expert_kernel, tm=tm),
        grid_spec=pltpu.PrefetchScalarGridSpec(
            num_scalar_prefetch=3,
            grid=(n_tiles,),
            in_specs=[
                pl.BlockSpec(memory_space=pl.ANY),
                pl.BlockSpec((None, None, d, f), lambda g, te, act, tok: (layer, te[g], 0, 0)),
                pl.BlockSpec((None, None, d, f), lambda g, te, act, tok: (layer, te[g], 0, 0)),
                pl.BlockSpec((None, None, f, d), lambda g, te, act, tok: (layer, te[g], 0, 0)),
            ],
            out_specs=pl.BlockSpec((tm, d), lambda g, te, act, tok: (g, 0)),
            scratch_shapes=[
                pltpu.VMEM((tm, d), F32),
                pltpu.VMEM((d, f), BF16), pltpu.VMEM((d, f), BF16), pltpu.VMEM((f, d), BF16),
                pltpu.SemaphoreType.DMA(()),
            ],
        ),
        out_shape=jax.ShapeDtypeStruct((n_tiles * tm, d), F32),
        compiler_params=pltpu.CompilerParams(
            dimension_semantics=("arbitrary",), vmem_limit_bytes=VMEM_LIMIT),
        name="expert_mlp",
    )(te, act, tok, x, w_gate, w_up, w_down)


def _combine_ln_kernel(dest_ref, r_ref, x_ref, g_ref, b_ref, y_hbm, o_ref, ob_ref, ybuf, sem,
                       *, tm, alpha):
    i = pl.program_id(0)
    _gather_rows(dest_ref, i * tm * TOP_K_EXPERTS, tm * TOP_K_EXPERTS, y_hbm, ybuf, sem)
    y0 = ybuf[0:tm, :]
    y1 = ybuf[tm:2 * tm, :]
    r = r_ref[...]
    moe = r[:, 2:3] * y0 + r[:, 3:4] * y1
    y = _layernorm(alpha * x_ref[...] + moe, g_ref[...], b_ref[...])
    o_ref[...] = y
    ob_ref[...] = y.astype(BF16)


def _combine_ln(dest, rinfo, x, g, b, y_sorted, layer, alpha):
    nt, d = x.shape
    tm = ROW_TILE
    dest = dest.reshape(nt // tm, tm, TOP_K_EXPERTS).transpose(0, 2, 1).reshape(-1)
    return pl.pallas_call(
        functools.partial(_combine_ln_kernel, tm=tm, alpha=alpha),
        grid_spec=pltpu.PrefetchScalarGridSpec(
            num_scalar_prefetch=1,
            grid=(nt // tm,),
            in_specs=[
                pl.BlockSpec((tm, LANES), lambda i, dest: (i, 0)),
                pl.BlockSpec((tm, d), lambda i, dest: (i, 0)),
                pl.BlockSpec((None, 1, d), lambda i, dest: (layer, 0, 0)),
                pl.BlockSpec((None, 1, d), lambda i, dest: (layer, 0, 0)),
                pl.BlockSpec(memory_space=pl.ANY),
            ],
            out_specs=[pl.BlockSpec((tm, d), lambda i, dest: (i, 0)),
                       pl.BlockSpec((tm, d), lambda i, dest: (i, 0))],
            scratch_shapes=[pltpu.VMEM((tm * TOP_K_EXPERTS, d), F32), pltpu.SemaphoreType.DMA(())],
        ),
        out_shape=[jax.ShapeDtypeStruct((nt, d), F32), jax.ShapeDtypeStruct((nt, d), BF16)],
        compiler_params=pltpu.CompilerParams(
            dimension_semantics=("arbitrary",), vmem_limit_bytes=VMEM_LIMIT),
        name="combine_ln",
    )(dest, rinfo, x, g, b, y_sorted)


def _rope_tables(pos):
    half = HEAD_DIM // 2
    inv_freq = ROPE_THETA ** (-jnp.arange(half, dtype=F32) / half)
    ang = pos.astype(F32)[:, None] * inv_freq[None, :]
    cos, sin = jnp.cos(ang), jnp.sin(ang)
    return jnp.concatenate([cos, cos], axis=-1), jnp.concatenate([-sin, sin], axis=-1)


def kernel(x_prompt, x_sample, cache_k, cache_v, page_table, w_qkv, w_o, ln_mix_g, ln_mix_b,
           w_group, b_group, w_route, b_route, w_gate, w_up, w_down, ln_ffn_g, ln_ffn_b):
    batch, seq, d = x_prompt.shape
    db, ts, _ = x_sample.shape
    depth = w_qkv.shape[0]
    heads = d // HEAD_DIM
    n_pool, page = cache_k.shape[1], cache_k.shape[2]
    n_pages = page_table.shape[1]
    past = n_pages * page
    ng, ne = w_group.shape[-1], w_route.shape[-1]
    epg = ne // ng
    assert MOBA_BLOCK % page == 0 and past % MOBA_BLOCK == 0 and ts <= page
    assert ng + ne <= LANES
    alpha = (2 * depth) ** 0.25

    n_p, n_s = batch * seq, db * ts
    nt = n_p + -(-n_s // ROW_TILE) * ROW_TILE
    pad = nt - n_p - n_s

    pos = jnp.concatenate([jnp.tile(jnp.arange(seq, dtype=jnp.int32), batch),
                           jnp.tile(past + jnp.arange(ts, dtype=jnp.int32), db),
                           jnp.zeros((pad,), jnp.int32)])
    cos, sin = _rope_tables(pos)
    ones = jnp.ones_like(cos)
    zeros = jnp.zeros_like(sin)

    x = jnp.concatenate([x_prompt.reshape(n_p, d), x_sample.reshape(n_s, d),
                         jnp.zeros((pad, d), F32)], axis=0)
    xb = x.astype(BF16)
    wqkv_bf = w_qkv.astype(BF16)
    wo_bf = w_o.astype(BF16)
    w_r = jnp.concatenate([w_group, w_route, jnp.zeros((depth, d, LANES - ng - ne), F32)], axis=-1)
    b_r = jnp.concatenate([b_group, b_route, jnp.zeros((depth, LANES - ng - ne), F32)], axis=-1)
    g_mix, b_mix = ln_mix_g.reshape(depth, 1, d), ln_mix_b.reshape(depth, 1, d)
    g_ffn, b_ffn = ln_ffn_g.reshape(depth, 1, d), ln_ffn_b.reshape(depth, 1, d)
    cache_k2 = cache_k.reshape(depth, n_pool, page * heads, HEAD_DIM)
    cache_v2 = cache_v.reshape(depth, n_pool, page * heads, HEAD_DIM)

    def new_page(a):
        a = a.reshape(db, ts * heads, HEAD_DIM)
        return jnp.pad(a, ((0, 0), (0, (page - ts) * heads), (0, 0)))

    kp, vp, ks, vs = [], [], [], []
    for layer in range(depth):
        kind = layer % 2
        qkv = _qkv_proj(xb, wqkv_bf, cos if kind == 0 else ones, sin if kind == 0 else zeros,
                        layer, rope=(kind == 0))
        o_p = _prompt_attn(qkv, kind, batch, seq)
        q_s = qkv[0, n_p:n_p + n_s].reshape(db, ts, d)
        k_s, v_s = qkv[1, n_p:n_p + n_s], qkv[2, n_p:n_p + n_s]
        bias = None
        if kind == 0:
            means = _cache_means(cache_k2, page_table, layer, page, heads)
            bias = _moba_select(q_s, means.reshape(db, means.shape[1], d), heads)
        o_s = _sample_attn(q_s, new_page(k_s), new_page(v_s), bias, cache_k2, cache_v2,
                           page_table, layer, kind, heads, page)
        o_s = o_s.reshape(db, heads, ts, HEAD_DIM).transpose(0, 2, 1, 3).reshape(n_s, d)
        o = jnp.concatenate([o_p, o_s.astype(BF16), jnp.zeros((pad, d), BF16)], axis=0)
        x1, _ = _proj_ln(o, wo_bf, x, g_mix, b_mix, layer, alpha)

        rinfo = _router(x1, w_r[layer], b_r[layer:layer + 1], ng, epg)
        e_idx = rinfo[:, :TOP_K_EXPERTS].astype(jnp.int32)
        dest, tok, te, act, n_tiles = _route_plan(e_idx, ne, ROW_TILE)
        y_sorted = _expert_mlp(x1, tok, te, act, n_tiles, w_gate, w_up, w_down, layer, ROW_TILE)
        x, xb = _combine_ln(dest, rinfo, x1, g_ffn, b_ffn, y_sorted, layer, alpha)

        kp.append(qkv[1, :n_p].reshape(batch, seq, heads, HEAD_DIM))
        vp.append(qkv[2, :n_p].reshape(batch, seq, heads, HEAD_DIM))
        ks.append(k_s.reshape(db, ts, heads, HEAD_DIM))
        vs.append(v_s.reshape(db, ts, heads, HEAD_DIM))

    return (x[:n_p].reshape(batch, seq, d), x[n_p:n_p + n_s].reshape(db, ts, d),
            jnp.stack(kp), jnp.stack(vp), jnp.stack(ks), jnp.stack(vs))
```

```python
import functools

import jax
import jax.numpy as jnp
from jax import lax
from jax.experimental import pallas as pl
from jax.experimental.pallas import tpu as pltpu

HEAD_DIM = 128
MOBA_BLOCK = 256
MOBA_TOP_K = 3
ROPE_THETA = 10000.0
LN_EPS = 1e-5
TOP_K_EXPERTS = 2
LANES = 128
SUBLANES = 8
BF16_ROWS = 16
ROW_TILE = 256
NEG = -1e30
VMEM_LIMIT = 48 * 1024 * 1024

F32 = jnp.float32
BF16 = jnp.bfloat16
NT_DIMS = (((1,), (1,)), ((), ()))
NN_DIMS = (((1,), (0,)), ((), ()))


def _split_bf16(a):
    hi = a.astype(BF16)
    lo = (a - hi.astype(F32)).astype(BF16)
    return hi, lo


def _dot3(a, b, dims):
    ah, al = _split_bf16(a)
    bh, bl = _split_bf16(b)
    d = lambda x, y: lax.dot_general(x, y, dims, preferred_element_type=F32)
    return d(ah, bh) + d(ah, bl) + d(al, bh)


def _dot(a, b, dims=NN_DIMS):
    return lax.dot_general(a, b, dims, preferred_element_type=F32)


def _pick_tile(n, target, quantum):
    best = quantum
    t = quantum
    while t <= target:
        if n % t == 0:
            best = t
        t += quantum
    assert n % best == 0, (n, quantum)
    return best


def _layernorm(h, g, b):
    mu = jnp.mean(h, axis=-1, keepdims=True)
    d = h - mu
    var = jnp.mean(d * d, axis=-1, keepdims=True)
    return d * lax.rsqrt(var + LN_EPS) * g + b


def _log_sigmoid_pair(z):
    sp = jnp.log(1.0 + jnp.exp(-jnp.abs(z)))
    pos = jnp.minimum(z, 0.0) - sp
    return pos, pos - z


def _block_rank(g, n_blocks, n_valid):
    blk = lax.broadcasted_iota(jnp.int32, g.shape, 0)
    rank = jnp.zeros(g.shape, jnp.int32)
    for n in range(n_blocks):
        gn = g[n:n + 1, :]
        beats = (gn > g) | ((gn == g) & (blk > n))
        inc = jnp.where(beats, 1, 0)
        if n_valid is not None:
            inc = inc * (n < n_valid).astype(jnp.int32)
        rank = rank + inc
    return rank, blk


def _topk_bias(g, n_blocks, n_valid):
    rank, blk = _block_rank(g, n_blocks, n_valid)
    sel = (rank < MOBA_TOP_K) & (blk < n_valid)
    return jnp.where(sel, 0.0, NEG)


def _qkv_kernel(x_ref, w_ref, cos_ref, sin_ref, o_ref, *, rope_tiles):
    acc = _dot(x_ref[...], w_ref[...])
    if rope_tiles == 0:
        o_ref[...] = acc
        return
    j = pl.program_id(0)

    @pl.when(j < rope_tiles)
    def _():
        cos = cos_ref[...]
        sin = sin_ref[...]
        for c in range(acc.shape[1] // HEAD_DIM):
            blk = acc[:, c * HEAD_DIM:(c + 1) * HEAD_DIM]
            rot = pltpu.roll(blk, HEAD_DIM // 2, axis=1)
            o_ref[:, c * HEAD_DIM:(c + 1) * HEAD_DIM] = blk * cos + rot * sin

    @pl.when(j >= rope_tiles)
    def _():
        o_ref[...] = acc


def _qkv_proj(xb, w_bf, cos, sin, layer, rope):
    nt, d = xb.shape
    tm = _pick_tile(nt, 768, ROW_TILE)
    tn = min(512, d)
    ncol = d // tn
    return pl.pallas_call(
        functools.partial(_qkv_kernel, rope_tiles=2 * ncol if rope else 0),
        grid=(3 * ncol, nt // tm),
        in_specs=[
            pl.BlockSpec((tm, d), lambda j, i: (i, 0)),
            pl.BlockSpec((None, d, tn), lambda j, i: (layer, 0, j)),
            pl.BlockSpec((tm, HEAD_DIM), lambda j, i: (i, 0)),
            pl.BlockSpec((tm, HEAD_DIM), lambda j, i: (i, 0)),
        ],
        out_specs=pl.BlockSpec((None, tm, tn), lambda j, i: (j // ncol, i, j % ncol)),
        out_shape=jax.ShapeDtypeStruct((3, nt, d), F32),
        compiler_params=pltpu.CompilerParams(
            dimension_semantics=("parallel", "parallel"), vmem_limit_bytes=VMEM_LIMIT),
        name="qkv_proj",
    )(xb, w_bf, cos, sin)


def _prompt_attn_kernel(q_ref, k_ref, v_ref, o_ref, kb_ref, vt_ref, acc_ref, *scratch, kind, nb, hps):
    blk = MOBA_BLOCK
    i = pl.program_id(2)
    scale = HEAD_DIM ** -0.5
    cols = lambda hh: slice(hh * HEAD_DIM, (hh + 1) * HEAD_DIM)

    @pl.when(i == 0)
    def _():
        if kind == 0:
            scratch[0][...] = jnp.zeros(scratch[0].shape, F32)
        for hh in range(hps):
            for j in range(nb):
                kj = k_ref[j * blk:(j + 1) * blk, cols(hh)]
                kb_ref[hh * nb + j] = kj.astype(BF16)
                vt_ref[hh * nb + j] = v_ref[j * blk:(j + 1) * blk, cols(hh)].T.astype(BF16)
                if kind == 0:
                    scratch[0][hh, j:j + 1, :] = jnp.sum(kj, axis=0, keepdims=True) * (1.0 / blk)

    q = [q_ref[:, cols(hh)] for hh in range(hps)]
    qs = [(x * scale).astype(BF16) for x in q]
    key_i = lax.broadcasted_iota(jnp.int32, (blk, blk), 0)
    qry_i = lax.broadcasted_iota(jnp.int32, (blk, blk), 1)

    def scores(hh, j):
        return _dot(kb_ref[hh * nb + j], qs[hh], NT_DIMS)

    st_ref = scratch[-1]
    heads_r = range(hps)
    if kind == 0:
        means_ref, bias_ref = scratch[0], scratch[1]
        for hh in heads_r:
            gate = _dot3(means_ref[hh], q[hh], NT_DIMS)
            bias_ref[hh] = _topk_bias(gate, nb, i)
        ss = [jnp.where(key_i <= qry_i, scores(hh, i), NEG) for hh in heads_r]
        ms = [jnp.max(s, axis=0, keepdims=True) for s in ss]
        ps = [jnp.exp(s - m) for s, m in zip(ss, ms)]
        pvs = [_dot(vt_ref[hh * nb + i], ps[hh].astype(BF16)) for hh in heads_r]
        for hh in heads_r:
            st_ref[hh, 0:1, :] = ms[hh]
            st_ref[hh, 1:2, :] = jnp.sum(ps[hh], axis=0, keepdims=True)
            acc_ref[hh] = pvs[hh]

        for j in range(nb - 1):
            @pl.when(j < i)
            def _(j=j):
                ss = [scores(hh, j) + bias_ref[hh, j:j + 1, :] for hh in heads_r]
                m_old = [st_ref[hh, 0:1, :] for hh in heads_r]
                m_new = [jnp.maximum(m_old[hh], jnp.max(ss[hh], axis=0, keepdims=True)) for hh in heads_r]
                ps = [jnp.exp(ss[hh] - m_new[hh]) for hh in heads_r]
                pvs = [_dot(vt_ref[hh * nb + j], ps[hh].astype(BF16)) for hh in heads_r]
                for hh in heads_r:
                    a = jnp.exp(m_old[hh] - m_new[hh])
                    st_ref[hh, 0:1, :] = m_new[hh]
                    st_ref[hh, 1:2, :] = a * st_ref[hh, 1:2, :] + jnp.sum(ps[hh], axis=0, keepdims=True)
                    acc_ref[hh] = a * acc_ref[hh] + pvs[hh]

        for hh in heads_r:
            o_ref[:, cols(hh)] = (acc_ref[hh] / st_ref[hh, 1:2, :]).T.astype(o_ref.dtype)
    else:
        tri = jnp.where(qry_i > key_i, 1.0, 0.0).astype(BF16)

        def sb_step(j, cs, mask):
            zs = [scores(hh, j) for hh in heads_r]
            pairs = [_log_sigmoid_pair(z) for z in zs]
            stays = [jnp.where(mask, p[1], 0.0) if mask is not None else p[1] for p in pairs]
            splits = [_split_bf16(s) for s in stays]
            afters = [_dot(tri, hi) + _dot(tri, lo) for hi, lo in splits]
            ws = [jnp.exp(pairs[hh][0] + afters[hh] + cs[hh]) for hh in heads_r]
            if mask is not None:
                ws = [jnp.where(mask, w, 0.0) for w in ws]
            pvs = [_dot(vt_ref[hh * nb + j], ws[hh].astype(BF16)) for hh in heads_r]
            return pvs, [jnp.sum(s, axis=0, keepdims=True) for s in stays]

        pvs, csums = sb_step(i, [0.0] * hps, key_i < qry_i)
        for hh in heads_r:
            st_ref[hh, 0:1, :] = csums[hh]
            acc_ref[hh] = pvs[hh]

        for j in reversed(range(nb - 1)):
            @pl.when(j < i)
            def _(j=j):
                cs = [st_ref[hh, 0:1, :] for hh in heads_r]
                pvs, csums = sb_step(j, cs, None)
                for hh in heads_r:
                    acc_ref[hh] = acc_ref[hh] + pvs[hh]
                    st_ref[hh, 0:1, :] = cs[hh] + csums[hh]

        for hh in heads_r:
            o_ref[:, cols(hh)] = acc_ref[hh].T.astype(o_ref.dtype)


def _prompt_attn(qkv, kind, batch, seq):
    d = qkv.shape[2]
    heads = d // HEAD_DIM
    blk = MOBA_BLOCK
    assert seq % blk == 0
    nb = seq // blk
    hps = 4 if heads % 4 == 0 else (2 if heads % 2 == 0 else 1)
    w = hps * HEAD_DIM
    scratch = [pltpu.VMEM((hps * nb, blk, HEAD_DIM), BF16), pltpu.VMEM((hps * nb, HEAD_DIM, blk), BF16),
               pltpu.VMEM((hps, HEAD_DIM, blk), F32)]
    if kind == 0:
        nbp = -(-nb // SUBLANES) * SUBLANES
        scratch += [pltpu.VMEM((hps, nbp, HEAD_DIM), F32), pltpu.VMEM((hps, nbp, blk), F32)]
    scratch.append(pltpu.VMEM((hps, SUBLANES, blk), F32))
    return pl.pallas_call(
        functools.partial(_prompt_attn_kernel, kind=kind, nb=nb, hps=hps),
        grid=(batch, heads // hps, nb),
        in_specs=[
            pl.BlockSpec((None, blk, w), lambda b, h, i: (0, b * nb + i, h)),
            pl.BlockSpec((None, seq, w), lambda b, h, i: (1, b, h)),
            pl.BlockSpec((None, seq, w), lambda b, h, i: (2, b, h)),
        ],
        out_specs=pl.BlockSpec((blk, w), lambda b, h, i: (b * nb + i, h)),
        out_shape=jax.ShapeDtypeStruct((batch * seq, d), BF16),
        scratch_shapes=scratch,
        compiler_params=pltpu.CompilerParams(
            dimension_semantics=("parallel", "parallel", "arbitrary"), vmem_limit_bytes=VMEM_LIMIT),
        name="prompt_attn_moba" if kind == 0 else "prompt_attn_sb",
    )(qkv, qkv, qkv)


def _cache_means_kernel(pt_ref, *refs, bps, ppb, page, heads):
    o_ref = refs[bps * ppb]
    for j in range(bps):
        tot = jnp.zeros((heads, HEAD_DIM), F32)
        for u in range(ppb):
            tot = tot + jnp.sum(refs[j * ppb + u][...].reshape(page, heads, HEAD_DIM), axis=0)
        o_ref[j] = tot * (1.0 / (page * ppb))


def _cache_means(cache2, page_table, layer, page, heads):
    db, n_pages = page_table.shape
    ppb = MOBA_BLOCK // page
    nbp = n_pages // ppb
    bps = 4 if nbp % 4 == 0 else 1
    rows = page * heads

    def page_spec(j, u):
        return pl.BlockSpec((None, None, rows, HEAD_DIM),
                            lambda b, n, pt: (layer, pt[b, (n * bps + j) * ppb + u], 0, 0))

    return pl.pallas_call(
        functools.partial(_cache_means_kernel, bps=bps, ppb=ppb, page=page, heads=heads),
        grid_spec=pltpu.PrefetchScalarGridSpec(
            num_scalar_prefetch=1,
            grid=(db, nbp // bps),
            in_specs=[page_spec(j, u) for j in range(bps) for u in range(ppb)],
            out_specs=pl.BlockSpec((None, bps, heads, HEAD_DIM), lambda b, n, pt: (b, n, 0, 0)),
        ),
        out_shape=jax.ShapeDtypeStruct((db, nbp, heads, HEAD_DIM), F32),
        compiler_params=pltpu.CompilerParams(
            dimension_semantics=("parallel", "parallel"), vmem_limit_bytes=VMEM_LIMIT),
        name="cache_means",
    )(page_table, *([cache2] * (bps * ppb)))


def _block_diag_queries(q, heads, ts):
    rep = jnp.concatenate([q] * heads, axis=0) if heads > 1 else q
    r = lax.broadcasted_iota(jnp.int32, rep.shape, 0) // ts
    c = lax.broadcasted_iota(jnp.int32, rep.shape, 1) // HEAD_DIM
    return jnp.where(r == c, rep, 0.0)


def _moba_select_kernel(q_ref, means_ref, idx_ref, *, heads, ts, nbp):
    qbd = _block_diag_queries(q_ref[...], heads, ts)
    gate = _dot3(means_ref[...], qbd, NT_DIMS)
    rank, blk = _block_rank(gate, nbp, None)
    blk_f = blk.astype(F32)
    rows = [jnp.sum(jnp.where(rank == k, blk_f, 0.0), axis=0, keepdims=True) for k in range(MOBA_TOP_K)]
    rows.append(jnp.zeros((SUBLANES - MOBA_TOP_K, heads * ts), F32))
    idx_ref[...] = jnp.concatenate(rows, axis=0).astype(jnp.int32)


def _moba_select(q_s, means, heads):
    db, ts, d = q_s.shape
    nbp = means.shape[1]
    assert nbp >= MOBA_TOP_K
    return pl.pallas_call(
        functools.partial(_moba_select_kernel, heads=heads, ts=ts, nbp=nbp),
        grid=(db,),
        in_specs=[pl.BlockSpec((None, ts, d), lambda b: (b, 0, 0)),
                  pl.BlockSpec((None, nbp, d), lambda b: (b, 0, 0))],
        out_specs=pl.BlockSpec((None, SUBLANES, heads * ts), lambda b: (b, 0, 0)),
        out_shape=jax.ShapeDtypeStruct((db, SUBLANES, heads * ts), jnp.int32),
        compiler_params=pltpu.CompilerParams(
            dimension_semantics=("parallel",), vmem_limit_bytes=VMEM_LIMIT),
        name="moba_select",
    )(q_s, means)


def _sample_moba_kernel(idx_ref, pt_ref, q_ref, knew_ref, vnew_ref, ck_hbm, cv_hbm, o_ref,
                        kbuf, vbuf, sem, *, layer, heads, ts, page, ppb):
    b = pl.program_id(0)
    h = pl.program_id(1)
    g = b * heads + h
    total = pl.num_programs(0) * heads
    slot = g % 2
    ht = heads * ts
    scale = HEAD_DIM ** -0.5

    def block_copies(bb, hh, sl):
        cps = []
        for t in range(ts):
            for k in range(MOBA_TOP_K):
                blk = idx_ref[(bb * SUBLANES + k) * ht + hh * ts + t]
                f = t * MOBA_TOP_K + k
                for u in range(ppb):
                    pg = pt_ref[bb, blk * ppb + u]
                    rows = pl.ds(u * page, page)
                    cps.append(pltpu.make_async_copy(
                        ck_hbm.at[layer, pg, :, hh, :], kbuf.at[sl, f, rows], sem.at[0, sl]))
                    cps.append(pltpu.make_async_copy(
                        cv_hbm.at[layer, pg, :, hh, :], vbuf.at[sl, f, rows], sem.at[1, sl]))
        return cps

    @pl.when(g == 0)
    def _():
        for cp in block_copies(b, h, slot):
            cp.start()

    @pl.when(g + 1 < total)
    def _():
        g1 = g + 1
        for cp in block_copies(g1 // heads, g1 % heads, 1 - slot):
            cp.start()

    for cp in block_copies(b, h, slot):
        cp.wait()

    q = (q_ref[...] * scale).astype(BF16)
    row = lax.broadcasted_iota(jnp.int32, (BF16_ROWS, page), 0)
    key = lax.broadcasted_iota(jnp.int32, (BF16_ROWS, page), 1)
    row_o = lax.broadcasted_iota(jnp.int32, (BF16_ROWS, HEAD_DIM), 0)
    s_own = jnp.where(key <= row, _dot(q, knew_ref[...].astype(BF16), NT_DIMS), NEG)
    m_own = jnp.max(s_own, axis=1, keepdims=True)
    vn = vnew_ref[...].astype(BF16)
    nf = ts * MOBA_TOP_K
    ss = [_dot(q, kbuf[slot, f].astype(BF16), NT_DIMS) for f in range(nf)]
    out = jnp.zeros((BF16_ROWS, HEAD_DIM), F32)
    for t in range(ts):
        fs = [t * MOBA_TOP_K + k for k in range(MOBA_TOP_K)]
        m = m_own
        for f in fs:
            m = jnp.maximum(m, jnp.max(ss[f], axis=1, keepdims=True))
        p = jnp.exp(s_own - m)
        l = jnp.sum(p, axis=1, keepdims=True)
        acc = _dot(p.astype(BF16), vn)
        for f in fs:
            p = jnp.exp(ss[f] - m)
            l = l + jnp.sum(p, axis=1, keepdims=True)
            acc = acc + _dot(p.astype(BF16), vbuf[slot, f].astype(BF16))
        out = jnp.where(row_o == t, acc / l, out)
    o_ref[...] = out


def _sample_moba(idx, page_table, q16, knew, vnew, cache_k, cache_v, layer):
    db, heads = q16.shape[0], q16.shape[1]
    page = cache_k.shape[2]
    ts = idx.shape[0] // (db * SUBLANES * heads)
    ppb = MOBA_BLOCK // page
    nf = ts * MOBA_TOP_K
    per_bh = lambda rows: pl.BlockSpec((None, None, rows, HEAD_DIM), lambda b, h, idx, pt: (b, h, 0, 0))
    return pl.pallas_call(
        functools.partial(_sample_moba_kernel, layer=layer, heads=heads, ts=ts, page=page, ppb=ppb),
        grid_spec=pltpu.PrefetchScalarGridSpec(
            num_scalar_prefetch=2,
            grid=(db, heads),
            in_specs=[per_bh(BF16_ROWS), per_bh(page), per_bh(page),
                      pl.BlockSpec(memory_space=pl.ANY), pl.BlockSpec(memory_space=pl.ANY)],
            out_specs=per_bh(BF16_ROWS),
            scratch_shapes=[pltpu.VMEM((2, nf, MOBA_BLOCK, HEAD_DIM), F32),
                            pltpu.VMEM((2, nf, MOBA_BLOCK, HEAD_DIM), F32),
                            pltpu.SemaphoreType.DMA((2, 2))],
        ),
        out_shape=jax.ShapeDtypeStruct((db, heads, BF16_ROWS, HEAD_DIM), F32),
        compiler_params=pltpu.CompilerParams(
            dimension_semantics=("arbitrary", "arbitrary"), vmem_limit_bytes=VMEM_LIMIT),
        name="sample_attn_moba",
    )(idx, page_table, q16, knew, vnew, cache_k, cache_v)


def _sample_sb_kernel(pt_ref, q_ref, knew_ref, vnew_ref, ck_hbm, cv_hbm, o_ref,
                      kbuf, vbuf, sem, q16_ref, acc_ref, c_ref, *, layer, heads, ts, page, n_pages, pp):
    b = pl.program_id(0)
    s = pl.program_id(1)
    n_steps = n_pages // pp
    g = b * n_steps + s
    total = pl.num_programs(0) * n_steps
    slot = g % 2
    ht = heads * ts
    grp = BF16_ROWS // ts
    scale = HEAD_DIM ** -0.5

    def page_copies(bb, ss, sl):
        cps = []
        for u in range(pp):
            pg = pt_ref[bb, n_pages - 1 - (ss * pp + u)]
            for h in range(heads):
                cps.append(pltpu.make_async_copy(
                    ck_hbm.at[layer, pg, :, h, :], kbuf.at[sl, u, h], sem.at[0, sl]))
                cps.append(pltpu.make_async_copy(
                    cv_hbm.at[layer, pg, :, h, :], vbuf.at[sl, u, h], sem.at[1, sl]))
        return cps

    @pl.when(g == 0)
    def _():
        for cp in page_copies(b, s, slot):
            cp.start()

    @pl.when(g + 1 < total)
    def _():
        g1 = g + 1
        for cp in page_copies(g1 // n_steps, g1 % n_steps, 1 - slot):
            cp.start()

    row_t = lax.broadcasted_iota(jnp.int32, (ht, page), 0) % ts
    lane = lax.broadcasted_iota(jnp.int32, (ht, page), 1)
    grp_row = lax.broadcasted_iota(jnp.int32, (BF16_ROWS, page), 0) // ts
    tri = jnp.where(lax.broadcasted_iota(jnp.int32, (page, page), 0)
                    > lax.broadcasted_iota(jnp.int32, (page, page), 1), 1.0, 0.0).astype(BF16)

    def scores(k_at):
        parts = []
        for gi in range(heads // grp):
            acc = None
            for hh in range(grp):
                h = gi * grp + hh
                z = _dot(q16_ref[h], k_at(h).astype(BF16), NT_DIMS)
                acc = z if acc is None else acc + z
            parts.append(acc)
        return jnp.concatenate(parts, axis=0)

    def weighted_values(a, v_at):
        parts = []
        for gi in range(heads // grp):
            a16 = a[gi * BF16_ROWS:(gi + 1) * BF16_ROWS, :]
            acc = None
            for hh in range(grp):
                h = gi * grp + hh
                w = jnp.where(grp_row == hh, a16, 0.0).astype(BF16)
                o = _dot(w, v_at(h).astype(BF16))
                acc = o if acc is None else acc + o
            parts.append(acc)
        return jnp.concatenate(parts, axis=0)

    def process(pages, c, acc):
        pairs = [_log_sigmoid_pair(scores(k_at)) for k_at, _, _ in pages]
        stays = [jnp.where(m, p[1], 0.0) if m is not None else p[1] for p, (_, _, m) in zip(pairs, pages)]
        splits = [_split_bf16(s) for s in stays]
        afters = [_dot(hi, tri) + _dot(lo, tri) for hi, lo in splits]
        for u, (_, v_at, mask) in enumerate(pages):
            a = jnp.exp(pairs[u][0] + afters[u] + c)
            if mask is not None:
                a = jnp.where(mask, a, 0.0)
            acc = acc + weighted_values(a, v_at)
            c = c + jnp.sum(stays[u], axis=1, keepdims=True)
        return c, acc

    @pl.when(s == 0)
    def _():
        qbd = _block_diag_queries(q_ref[...], heads, ts) * scale
        for h in range(heads):
            gi = h // grp
            q16_ref[h] = qbd[gi * BF16_ROWS:(gi + 1) * BF16_ROWS,
                             h * HEAD_DIM:(h + 1) * HEAD_DIM].astype(BF16)
        new_keys = (lambda h: knew_ref[h], lambda h: vnew_ref[h], lane < row_t)
        c, acc = process([new_keys], jnp.zeros((ht, 1), F32), jnp.zeros((ht, HEAD_DIM), F32))
        c_ref[...] = c
        acc_ref[...] = acc

    for cp in page_copies(b, s, slot):
        cp.wait()
    pages = [(lambda h, u=u: kbuf[slot, u, h], lambda h, u=u: vbuf[slot, u, h], None) for u in range(pp)]
    c, acc = process(pages, c_ref[...], acc_ref[...])
    c_ref[...] = c
    acc_ref[...] = acc

    @pl.when(s == n_steps - 1)
    def _():
        o_ref[...] = acc_ref[...]


def _sample_sb(q_s, knew, vnew, cache_k, cache_v, page_table, layer):
    db, ts, d = q_s.shape
    heads = d // HEAD_DIM
    page = cache_k.shape[2]
    n_pages = page_table.shape[1]
    pp = 4 if n_pages % 4 == 0 else 1
    ht = heads * ts
    assert BF16_ROWS % ts == 0 and heads % (BF16_ROWS // ts) == 0
    return pl.pallas_call(
        functools.partial(_sample_sb_kernel, layer=layer, heads=heads, ts=ts, page=page,
                          n_pages=n_pages, pp=pp),
        grid_spec=pltpu.PrefetchScalarGridSpec(
            num_scalar_prefetch=1,
            grid=(db, n_pages // pp),
            in_specs=[pl.BlockSpec((None, ts, d), lambda b, s, pt: (b, 0, 0)),
                      pl.BlockSpec((None, heads, page, HEAD_DIM), lambda b, s, pt: (b, 0, 0, 0)),
                      pl.BlockSpec((None, heads, page, HEAD_DIM), lambda b, s, pt: (b, 0, 0, 0)),
                      pl.BlockSpec(memory_space=pl.ANY), pl.BlockSpec(memory_space=pl.ANY)],
            out_specs=pl.BlockSpec((None, ht, HEAD_DIM), lambda b, s, pt: (b, 0, 0)),
            scratch_shapes=[pltpu.VMEM((2, pp, heads, page, HEAD_DIM), F32),
                            pltpu.VMEM((2, pp, heads, page, HEAD_DIM), F32),
                            pltpu.SemaphoreType.DMA((2, 2)),
                            pltpu.VMEM((heads, BF16_ROWS, HEAD_DIM), BF16),
                            pltpu.VMEM((ht, HEAD_DIM), F32),
                            pltpu.VMEM((ht, 1), F32)],
        ),
        out_shape=jax.ShapeDtypeStruct((db, ht, HEAD_DIM), F32),
        compiler_params=pltpu.CompilerParams(
            dimension_semantics=("arbitrary", "arbitrary"), vmem_limit_bytes=VMEM_LIMIT),
        name="sample_attn_sb",
    )(page_table, q_s, knew, vnew, cache_k, cache_v)


def _proj_ln_kernel(o_ref, w_ref, x_ref, g_ref, b_ref, y_ref, yb_ref, *, alpha):
    h = alpha * x_ref[...] + _dot(o_ref[...], w_ref[...])
    y = _layernorm(h, g_ref[...], b_ref[...])
    y_ref[...] = y
    yb_ref[...] = y.astype(BF16)


def _proj_ln(o, w_bf, x, g, b, layer, alpha):
    nt, d = x.shape
    tm = ROW_TILE
    return pl.pallas_call(
        functools.partial(_proj_ln_kernel, alpha=alpha),
        grid=(nt // tm,),
        in_specs=[
            pl.BlockSpec((tm, d), lambda i: (i, 0)),
            pl.BlockSpec((None, d, d), lambda i: (layer, 0, 0)),
            pl.BlockSpec((tm, d), lambda i: (i, 0)),
            pl.BlockSpec((None, 1, d), lambda i: (layer, 0, 0)),
            pl.BlockSpec((None, 1, d), lambda i: (layer, 0, 0)),
        ],
        out_specs=[pl.BlockSpec((tm, d), lambda i: (i, 0)), pl.BlockSpec((tm, d), lambda i: (i, 0))],
        out_shape=[jax.ShapeDtypeStruct((nt, d), F32), jax.ShapeDtypeStruct((nt, d), BF16)],
        compiler_params=pltpu.CompilerParams(
            dimension_semantics=("parallel",), vmem_limit_bytes=VMEM_LIMIT),
        name="proj_ln",
    )(o, w_bf, x, g, b)


def _router_kernel(x_ref, w_ref, b_ref, o_ref, *, ng, epg):
    logits = _dot3(x_ref[...], w_ref[...], NN_DIMS) + b_ref[...]
    lane = lax.broadcasted_iota(jnp.int32, logits.shape, 1).astype(F32)
    ninf = -jnp.inf

    def top(v):
        m = jnp.max(v, axis=1, keepdims=True)
        idx = jnp.min(jnp.where(v == m, lane, float(LANES)), axis=1, keepdims=True)
        return m, idx

    gl = jnp.where(lane < ng, logits, ninf)
    gmax, gidx = top(gl)
    g_p = 1.0 / jnp.sum(jnp.exp(gl - gmax), axis=1, keepdims=True)
    lo = ng + gidx * epg
    el = jnp.where((lane >= lo) & (lane < lo + epg), logits, ninf)
    m1, i1 = top(el)
    m2, i2 = top(jnp.where(lane == i1, ninf, el))
    t = jnp.exp(m2 - m1)
    w0 = g_p / (1.0 + t)
    w1 = g_p * t / (1.0 + t)
    o_ref[...] = jnp.where(lane == 0, i1 - ng,
                           jnp.where(lane == 1, i2 - ng,
                                     jnp.where(lane == 2, w0, jnp.where(lane == 3, w1, 0.0))))


def _router(x, w, b, ng, epg):
    nt, d = x.shape
    tm = ROW_TILE
    return pl.pallas_call(
        functools.partial(_router_kernel, ng=ng, epg=epg),
        grid=(nt // tm,),
        in_specs=[pl.BlockSpec((tm, d), lambda i: (i, 0)),
                  pl.BlockSpec((d, LANES), lambda i: (0, 0)),
                  pl.BlockSpec((1, LANES), lambda i: (0, 0))],
        out_specs=pl.BlockSpec((tm, LANES), lambda i: (i, 0)),
        out_shape=jax.ShapeDtypeStruct((nt, LANES), F32),
        compiler_params=pltpu.CompilerParams(
            dimension_semantics=("parallel",), vmem_limit_bytes=VMEM_LIMIT),
        name="router",
    )(x, w, b)


def _route_plan(e_idx, ne, tm):
    n2 = e_idx.size
    flat = e_idx.reshape(-1)
    onehot = (flat[:, None] == jnp.arange(ne, dtype=jnp.int32)[None, :]).astype(jnp.int32)
    csum = jnp.cumsum(onehot, axis=0)
    counts = csum[-1]
    pos_in_e = jnp.sum((csum - onehot) * onehot, axis=1)
    tiles_e = (counts + tm - 1) // tm
    tile_end = jnp.cumsum(tiles_e)
    tile_start = tile_end - tiles_e
    dest = jnp.sum(onehot * tile_start[None, :], axis=1) * tm + pos_in_e
    n_tiles = -(-n2 // tm) + ne
    gidx = jnp.arange(n_tiles, dtype=jnp.int32)
    active = gidx < tile_end[-1]
    te = jnp.sum((gidx[:, None] >= tile_end[None, :]).astype(jnp.int32), axis=1)
    last_e = jnp.max(jnp.where(counts > 0, jnp.arange(ne, dtype=jnp.int32), 0))
    te = jnp.where(active, te, last_e).astype(jnp.int32)
    tok = jnp.zeros((n_tiles * tm,), jnp.int32).at[dest].set(jnp.arange(n2, dtype=jnp.int32) // 2)
    return dest.astype(jnp.int32), tok, te, active.astype(jnp.int32), n_tiles


def _gather_rows_start(idx_ref, base, n, src_hbm, dst, sem):
    def start(r, carry):
        pltpu.make_async_copy(src_hbm.at[pl.ds(idx_ref[base + r], 1)], dst.at[pl.ds(r, 1)], sem).start()
        return carry

    lax.fori_loop(0, n, start, 0, unroll=8)


def _gather_rows_wait(n, src_hbm, dst, sem):
    def wait(r, carry):
        pltpu.make_async_copy(src_hbm.at[pl.ds(0, 1)], dst.at[pl.ds(r, 1)], sem).wait()
        return carry

    lax.fori_loop(0, n, wait, 0, unroll=8)


def _expert_kernel(te_ref, act_ref, tok_ref, x_hbm, wg_ref, wu_ref, wd_ref, y_ref,
                   xbuf, wgb, wub, wdb, sem, *, tm):
    g = pl.program_id(0)
    n_tiles = pl.num_programs(0)
    slot = g % 2
    nxt = jnp.minimum(g + 1, n_tiles - 1)

    @pl.when((g == 0) & (act_ref[0] == 1))
    def _():
        _gather_rows_start(tok_ref, 0, tm, x_hbm, xbuf.at[0], sem.at[0])

    @pl.when((g + 1 < n_tiles) & (act_ref[nxt] == 1))
    def _():
        _gather_rows_start(tok_ref, nxt * tm, tm, x_hbm, xbuf.at[1 - slot], sem.at[1 - slot])

    prev = te_ref[jnp.maximum(g - 1, 0)]

    @pl.when((g == 0) | (te_ref[g] != prev))
    def _():
        wgb[...] = wg_ref[...].astype(BF16)
        wub[...] = wu_ref[...].astype(BF16)
        wdb[...] = wd_ref[...].astype(BF16)

    @pl.when(act_ref[g] == 1)
    def _():
        _gather_rows_wait(tm, x_hbm, xbuf.at[slot], sem.at[slot])
        xb = xbuf[slot].astype(BF16)
        gate = _dot(xb, wgb[...])
        up = _dot(xb, wub[...])
        h = gate * jax.nn.sigmoid(gate) * up
        y_ref[...] = _dot(h.astype(BF16), wdb[...])

    @pl.when(act_ref[g] == 0)
    def _():
        y_ref[...] = jnp.zeros(y_ref.shape, F32)


def _expert_mlp(x, tok, te, act, n_tiles, w_gate, w_up, w_down, layer, tm):
    nt, d = x.shape
    f = w_gate.shape[-1]
    return pl.pallas_call(
        functools.partial(_expert_kernel, tm=tm),
        grid_spec=pltpu.PrefetchScalarGridSpec(
            num_scalar_prefetch=3,
            grid=(n_tiles,),
            in_specs=[
                pl.BlockSpec(memory_space=pl.ANY),
                pl.BlockSpec((None, None, d, f), lambda g, te, act, tok: (layer, te[g], 0, 0)),
                pl.BlockSpec((None, None, d, f), lambda g, te, act, tok: (layer, te[g], 0, 0)),
                pl.BlockSpec((None, None, f, d), lambda g, te, act, tok: (layer, te[g], 0, 0)),
            ],
            out_specs=pl.BlockSpec((tm, d), lambda g, te, act, tok: (g, 0)),
            scratch_shapes=[
                pltpu.VMEM((2, tm, d), F32),
                pltpu.VMEM((d, f), BF16), pltpu.VMEM((d, f), BF16), pltpu.VMEM((f, d), BF16),
                pltpu.SemaphoreType.DMA((2,)),
            ],
        ),
        out_shape=jax.ShapeDtypeStruct((n_tiles * tm, d), F32),
        compiler_params=pltpu.CompilerParams(
            dimension_semantics=("arbitrary",), vmem_limit_bytes=VMEM_LIMIT),
        name="expert_mlp",
    )(te, act, tok, x, w_gate, w_up, w_down)


def _combine_ln_kernel(dest_ref, r_ref, x_ref, g_ref, b_ref, y_hbm, o_ref, ob_ref, ybuf, sem,
                       *, tm, alpha):
    i = pl.program_id(0)
    n_tiles = pl.num_programs(0)
    slot = i % 2
    rows = tm * TOP_K_EXPERTS

    @pl.when(i == 0)
    def _():
        _gather_rows_start(dest_ref, 0, rows, y_hbm, ybuf.at[0], sem.at[0])

    @pl.when(i + 1 < n_tiles)
    def _():
        _gather_rows_start(dest_ref, (i + 1) * rows, rows, y_hbm, ybuf.at[1 - slot], sem.at[1 - slot])

    _gather_rows_wait(rows, y_hbm, ybuf.at[slot], sem.at[slot])
    y0 = ybuf[slot, 0:tm, :]
    y1 = ybuf[slot, tm:2 * tm, :]
    r = r_ref[...]
    moe = r[:, 2:3] * y0 + r[:, 3:4] * y1
    y = _layernorm(alpha * x_ref[...] + moe, g_ref[...], b_ref[...])
    o_ref[...] = y
    ob_ref[...] = y.astype(BF16)


def _combine_ln(dest, rinfo, x, g, b, y_sorted, layer, alpha):
    nt, d = x.shape
    tm = ROW_TILE
    dest = dest.reshape(nt // tm, tm, TOP_K_EXPERTS).transpose(0, 2, 1).reshape(-1)
    return pl.pallas_call(
        functools.partial(_combine_ln_kernel, tm=tm, alpha=alpha),
        grid_spec=pltpu.PrefetchScalarGridSpec(
            num_scalar_prefetch=1,
            grid=(nt // tm,),
            in_specs=[
                pl.BlockSpec((tm, LANES), lambda i, dest: (i, 0)),
                pl.BlockSpec((tm, d), lambda i, dest: (i, 0)),
                pl.BlockSpec((None, 1, d), lambda i, dest: (layer, 0, 0)),
                pl.BlockSpec((None, 1, d), lambda i, dest: (layer, 0, 0)),
                pl.BlockSpec(memory_space=pl.ANY),
            ],
            out_specs=[pl.BlockSpec((tm, d), lambda i, dest: (i, 0)),
                       pl.BlockSpec((tm, d), lambda i, dest: (i, 0))],
            scratch_shapes=[pltpu.VMEM((2, tm * TOP_K_EXPERTS, d), F32), pltpu.SemaphoreType.DMA((2,))],
        ),
        out_shape=[jax.ShapeDtypeStruct((nt, d), F32), jax.ShapeDtypeStruct((nt, d), BF16)],
        compiler_params=pltpu.CompilerParams(
            dimension_semantics=("arbitrary",), vmem_limit_bytes=VMEM_LIMIT),
        name="combine_ln",
    )(dest, rinfo, x, g, b, y_sorted)


def _rope_tables(pos):
    half = HEAD_DIM // 2
    inv_freq = ROPE_THETA ** (-jnp.arange(half, dtype=F32) / half)
    ang = pos.astype(F32)[:, None] * inv_freq[None, :]
    cos, sin = jnp.cos(ang), jnp.sin(ang)
    return jnp.concatenate([cos, cos], axis=-1), jnp.concatenate([-sin, sin], axis=-1)


def kernel(x_prompt, x_sample, cache_k, cache_v, page_table, w_qkv, w_o, ln_mix_g, ln_mix_b,
           w_group, b_group, w_route, b_route, w_gate, w_up, w_down, ln_ffn_g, ln_ffn_b):
    batch, seq, d = x_prompt.shape
    db, ts, _ = x_sample.shape
    depth = w_qkv.shape[0]
    heads = d // HEAD_DIM
    n_pool, page = cache_k.shape[1], cache_k.shape[2]
    n_pages = page_table.shape[1]
    past = n_pages * page
    ng, ne = w_group.shape[-1], w_route.shape[-1]
    epg = ne // ng
    assert MOBA_BLOCK % page == 0 and past % MOBA_BLOCK == 0 and ts <= page and ts <= BF16_ROWS
    assert ng + ne <= LANES
    alpha = (2 * depth) ** 0.25

    n_p, n_s = batch * seq, db * ts
    nt = n_p + -(-n_s // ROW_TILE) * ROW_TILE
    pad = nt - n_p - n_s

    pos = jnp.concatenate([jnp.tile(jnp.arange(seq, dtype=jnp.int32), batch),
                           jnp.tile(past + jnp.arange(ts, dtype=jnp.int32), db),
                           jnp.zeros((pad,), jnp.int32)])
    cos, sin = _rope_tables(pos)

    x = jnp.concatenate([x_prompt.reshape(n_p, d), x_sample.reshape(n_s, d),
                         jnp.zeros((pad, d), F32)], axis=0)
    xb = x.astype(BF16)
    wqkv_bf = w_qkv.astype(BF16)
    wo_bf = w_o.astype(BF16)
    w_r = jnp.concatenate([w_group, w_route, jnp.zeros((depth, d, LANES - ng - ne), F32)], axis=-1)
    b_r = jnp.concatenate([b_group, b_route, jnp.zeros((depth, LANES - ng - ne), F32)], axis=-1)
    g_mix, b_mix = ln_mix_g.reshape(depth, 1, d), ln_mix_b.reshape(depth, 1, d)
    g_ffn, b_ffn = ln_ffn_g.reshape(depth, 1, d), ln_ffn_b.reshape(depth, 1, d)
    cache_k2 = cache_k.reshape(depth, n_pool, page * heads, HEAD_DIM)

    def per_head(a, rows):
        a = a.reshape(db, ts, heads, HEAD_DIM).transpose(0, 2, 1, 3)
        return jnp.pad(a, ((0, 0), (0, 0), (0, rows - ts), (0, 0)))

    kp, vp, ks, vs = [], [], [], []
    for layer in range(depth):
        kind = layer % 2
        qkv = _qkv_proj(xb, wqkv_bf, cos, sin, layer, rope=(kind == 0))
        o_p = _prompt_attn(qkv, kind, batch, seq)
        q_s, k_s, v_s = (qkv[c, n_p:n_p + n_s] for c in range(3))
        knew, vnew = per_head(k_s, page), per_head(v_s, page)
        if kind == 0:
            means = _cache_means(cache_k2, page_table, layer, page, heads)
            idx = _moba_select(q_s.reshape(db, ts, d), means.reshape(db, means.shape[1], d), heads)
            o_s = _sample_moba(idx.reshape(-1), page_table, per_head(q_s, BF16_ROWS), knew, vnew,
                               cache_k, cache_v, layer)
            o_s = o_s[:, :, :ts]
        else:
            o_s = _sample_sb(q_s.reshape(db, ts, d), knew, vnew, cache_k, cache_v, page_table, layer)
            o_s = o_s.reshape(db, heads, ts, HEAD_DIM)
        o_s = o_s.transpose(0, 2, 1, 3).reshape(n_s, d)
        o = jnp.concatenate([o_p, o_s.astype(BF16), jnp.zeros((pad, d), BF16)], axis=0)
        x1, _ = _proj_ln(o, wo_bf, x, g_mix, b_mix, layer, alpha)

        rinfo = _router(x1, w_r[layer], b_r[layer:layer + 1], ng, epg)
        e_idx = rinfo[:, :TOP_K_EXPERTS].astype(jnp.int32)
        dest, tok, te, act, n_tiles = _route_plan(e_idx, ne, ROW_TILE)
        y_sorted = _expert_mlp(x1, tok, te, act, n_tiles, w_gate, w_up, w_down, layer, ROW_TILE)
        x, xb = _combine_ln(dest, rinfo, x1, g_ffn, b_ffn, y_sorted, layer, alpha)

        kp.append(qkv[1, :n_p].reshape(batch, seq, heads, HEAD_DIM))
        vp.append(qkv[2, :n_p].reshape(batch, seq, heads, HEAD_DIM))
        ks.append(k_s.reshape(db, ts, heads, HEAD_DIM))
        vs.append(v_s.reshape(db, ts, heads, HEAD_DIM))

    return (x[:n_p].reshape(batch, seq, d), x[n_p:n_p + n_s].reshape(db, ts, d),
            jnp.stack(kp), jnp.stack(vp), jnp.stack(ks), jnp.stack(vs))
```

```python
import functools

import jax
import jax.numpy as jnp
from jax import lax
from jax.experimental import pallas as pl
from jax.experimental.pallas import tpu as pltpu

HEAD_DIM = 128
MOBA_BLOCK = 256
MOBA_TOP_K = 3
ROPE_THETA = 10000.0
LN_EPS = 1e-5
TOP_K_EXPERTS = 2
LANES = 128
SUBLANES = 8
BF16_ROWS = 16
ROW_TILE = 256
NEG = -1e30
SB_DEAD = -105.0
VMEM_LIMIT = 48 * 1024 * 1024

F32 = jnp.float32
BF16 = jnp.bfloat16
NT_DIMS = (((1,), (1,)), ((), ()))
NN_DIMS = (((1,), (0,)), ((), ()))


def _split_bf16(a):
    hi = a.astype(BF16)
    lo = (a - hi.astype(F32)).astype(BF16)
    return hi, lo


def _dot3(a, b, dims):
    ah, al = _split_bf16(a)
    bh, bl = _split_bf16(b)
    d = lambda x, y: lax.dot_general(x, y, dims, preferred_element_type=F32)
    return d(ah, bh) + d(ah, bl) + d(al, bh)


def _dot(a, b, dims=NN_DIMS):
    return lax.dot_general(a, b, dims, preferred_element_type=F32)


def _pick_tile(n, target, quantum):
    best = quantum
    t = quantum
    while t <= target:
        if n % t == 0:
            best = t
        t += quantum
    assert n % best == 0, (n, quantum)
    return best


def _layernorm(h, g, b):
    mu = jnp.mean(h, axis=-1, keepdims=True)
    d = h - mu
    var = jnp.mean(d * d, axis=-1, keepdims=True)
    return d * lax.rsqrt(var + LN_EPS) * g + b


def _log_sigmoid_pair(z):
    sp = jnp.log(1.0 + jnp.exp(-jnp.abs(z)))
    pos = jnp.minimum(z, 0.0) - sp
    return pos, pos - z


def _block_rank(g, n_blocks, n_valid):
    blk = lax.broadcasted_iota(jnp.int32, g.shape, 0)
    rank = jnp.zeros(g.shape, jnp.int32)
    for n in range(n_blocks):
        gn = g[n:n + 1, :]
        beats = (gn > g) | ((gn == g) & (blk > n))
        inc = jnp.where(beats, 1, 0)
        if n_valid is not None:
            inc = inc * (n < n_valid).astype(jnp.int32)
        rank = rank + inc
    return rank, blk


def _topk_bias(g, n_blocks, n_valid):
    rank, blk = _block_rank(g, n_blocks, n_valid)
    sel = (rank < MOBA_TOP_K) & (blk < n_valid)
    return jnp.where(sel, 0.0, NEG)


def _qkv_kernel(x_ref, w_ref, cos_ref, sin_ref, o_ref, *, rope_tiles):
    acc = _dot(x_ref[...], w_ref[...])
    if rope_tiles == 0:
        o_ref[...] = acc
        return
    j = pl.program_id(0)

    @pl.when(j < rope_tiles)
    def _():
        cos = cos_ref[...]
        sin = sin_ref[...]
        for c in range(acc.shape[1] // HEAD_DIM):
            blk = acc[:, c * HEAD_DIM:(c + 1) * HEAD_DIM]
            rot = pltpu.roll(blk, HEAD_DIM // 2, axis=1)
            o_ref[:, c * HEAD_DIM:(c + 1) * HEAD_DIM] = blk * cos + rot * sin

    @pl.when(j >= rope_tiles)
    def _():
        o_ref[...] = acc


def _qkv_proj(xb, w_bf, cos, sin, layer, rope):
    nt, d = xb.shape
    tm = _pick_tile(nt, 768, ROW_TILE)
    tn = min(512, d)
    ncol = d // tn
    return pl.pallas_call(
        functools.partial(_qkv_kernel, rope_tiles=2 * ncol if rope else 0),
        grid=(3 * ncol, nt // tm),
        in_specs=[
            pl.BlockSpec((tm, d), lambda j, i: (i, 0)),
            pl.BlockSpec((None, d, tn), lambda j, i: (layer, 0, j)),
            pl.BlockSpec((tm, HEAD_DIM), lambda j, i: (i, 0)),
            pl.BlockSpec((tm, HEAD_DIM), lambda j, i: (i, 0)),
        ],
        out_specs=pl.BlockSpec((None, tm, tn), lambda j, i: (j // ncol, i, j % ncol)),
        out_shape=jax.ShapeDtypeStruct((3, nt, d), F32),
        compiler_params=pltpu.CompilerParams(
            dimension_semantics=("parallel", "parallel"), vmem_limit_bytes=VMEM_LIMIT),
        name="qkv_proj",
    )(xb, w_bf, cos, sin)


def _head_proj_kernel(x_ref, w_ref, cos_ref, sin_ref, *rest, rope, per_head):
    o_ref = rest[-1]
    acc = _dot(x_ref[...], w_ref[...])
    for c in range(acc.shape[1] // HEAD_DIM):
        blk = acc[:, c * HEAD_DIM:(c + 1) * HEAD_DIM]
        if rope:
            blk = blk * cos_ref[...] + pltpu.roll(blk, HEAD_DIM // 2, axis=1) * sin_ref[...]
        if per_head:
            o_ref[:, c, :] = blk
        else:
            o_ref[:, c * HEAD_DIM:(c + 1) * HEAD_DIM] = blk


def _head_proj(xb, w_bf, cos, sin, layer, part, rope, n_rows, out_buf=None):
    d = xb.shape[1]
    heads = d // HEAD_DIM
    nh = SUBLANES if heads % SUBLANES == 0 else heads
    tn = nh * HEAD_DIM
    ncol = d // tn
    tm = _pick_tile(n_rows, 1024, ROW_TILE)
    in_specs = [
        pl.BlockSpec((tm, d), lambda j, i: (i, 0)),
        pl.BlockSpec((None, d, tn), lambda j, i: (layer, 0, part * ncol + j)),
        pl.BlockSpec((tm, HEAD_DIM), lambda j, i: (i, 0)),
        pl.BlockSpec((tm, HEAD_DIM), lambda j, i: (i, 0)),
    ]
    args = [xb, w_bf, cos, sin]
    if out_buf is None:
        out_specs = pl.BlockSpec((tm, tn), lambda j, i: (i, j))
        out_shape = jax.ShapeDtypeStruct((n_rows, d), F32)
        aliases = {}
    else:
        in_specs.append(pl.BlockSpec(memory_space=pl.ANY))
        args.append(out_buf)
        out_specs = pl.BlockSpec((None, tm, nh, HEAD_DIM), lambda j, i: (layer, i, j, 0))
        out_shape = jax.ShapeDtypeStruct(out_buf.shape, F32)
        aliases = {4: 0}
    return pl.pallas_call(
        functools.partial(_head_proj_kernel, rope=rope, per_head=out_buf is not None),
        grid=(ncol, n_rows // tm),
        in_specs=in_specs,
        out_specs=out_specs,
        out_shape=out_shape,
        input_output_aliases=aliases,
        compiler_params=pltpu.CompilerParams(
            dimension_semantics=("parallel", "parallel"), vmem_limit_bytes=VMEM_LIMIT),
        name="head_proj_" + "qkv"[part],
    )(*args)


def _prompt_attn_kernel(q_ref, k_hbm, v_hbm, o_ref, kst_ref, vst_ref, sem, kb_ref, vt_ref, acc_ref,
                        *scratch, kind, nb, hps, layer):
    blk = MOBA_BLOCK
    seq = nb * blk
    b = pl.program_id(0)
    hg = pl.program_id(1)
    i = pl.program_id(2)
    scale = HEAD_DIM ** -0.5
    cols = lambda hh: slice(hh * HEAD_DIM, (hh + 1) * HEAD_DIM)

    @pl.when(i == 0)
    def _():
        rows = pl.ds(pl.multiple_of(b * seq, blk), seq)
        cps = []
        for hh in range(hps):
            cps.append(pltpu.make_async_copy(k_hbm.at[layer, rows, hg * hps + hh, :], kst_ref.at[hh], sem.at[0]))
            cps.append(pltpu.make_async_copy(v_hbm.at[layer, rows, hg * hps + hh, :], vst_ref.at[hh], sem.at[1]))
        for cp in cps:
            cp.start()
        for cp in cps:
            cp.wait()
        if kind == 0:
            scratch[0][...] = jnp.zeros(scratch[0].shape, F32)
        for hh in range(hps):
            for j in range(nb):
                kj = kst_ref[hh, j * blk:(j + 1) * blk, :]
                kb_ref[hh * nb + j] = kj.astype(BF16)
                vt_ref[hh * nb + j] = vst_ref[hh, j * blk:(j + 1) * blk, :].T.astype(BF16)
                if kind == 0:
                    scratch[0][hh, j:j + 1, :] = jnp.sum(kj, axis=0, keepdims=True) * (1.0 / blk)

    q = [q_ref[:, cols(hh)] for hh in range(hps)]
    qs = [(x * scale).astype(BF16) for x in q]
    key_i = lax.broadcasted_iota(jnp.int32, (blk, blk), 0)
    qry_i = lax.broadcasted_iota(jnp.int32, (blk, blk), 1)

    def scores(hh, j):
        return _dot(kb_ref[hh * nb + j], qs[hh], NT_DIMS)

    st_ref, done_ref = scratch[-2], scratch[-1]
    heads_r = range(hps)
    if kind == 0:
        means_ref, bias_ref = scratch[0], scratch[1]
        for hh in heads_r:
            gate = _dot3(means_ref[hh], q[hh], NT_DIMS)
            bias_ref[hh] = _topk_bias(gate, nb, i)
        ss = [jnp.where(key_i <= qry_i, scores(hh, i), NEG) for hh in heads_r]
        ms = [jnp.max(s, axis=0, keepdims=True) for s in ss]
        ps = [jnp.exp(s - m) for s, m in zip(ss, ms)]
        pvs = [_dot(vt_ref[hh * nb + i], ps[hh].astype(BF16)) for hh in heads_r]
        for hh in heads_r:
            st_ref[hh, 0:1, :] = ms[hh]
            st_ref[hh, 1:2, :] = jnp.sum(ps[hh], axis=0, keepdims=True)
            acc_ref[hh] = pvs[hh]

        for j in range(nb - 1):
            @pl.when(j < i)
            def _(j=j):
                ss = [scores(hh, j) + bias_ref[hh, j:j + 1, :] for hh in heads_r]
                m_old = [st_ref[hh, 0:1, :] for hh in heads_r]
                m_new = [jnp.maximum(m_old[hh], jnp.max(ss[hh], axis=0, keepdims=True)) for hh in heads_r]
                ps = [jnp.exp(ss[hh] - m_new[hh]) for hh in heads_r]
                pvs = [_dot(vt_ref[hh * nb + j], ps[hh].astype(BF16)) for hh in heads_r]
                for hh in heads_r:
                    a = jnp.exp(m_old[hh] - m_new[hh])
                    st_ref[hh, 0:1, :] = m_new[hh]
                    st_ref[hh, 1:2, :] = a * st_ref[hh, 1:2, :] + jnp.sum(ps[hh], axis=0, keepdims=True)
                    acc_ref[hh] = a * acc_ref[hh] + pvs[hh]

        for hh in heads_r:
            o_ref[:, cols(hh)] = (acc_ref[hh] / st_ref[hh, 1:2, :]).T.astype(o_ref.dtype)
    else:
        tri = jnp.where(qry_i > key_i, 1.0, 0.0).astype(BF16)

        def sb_step(j, cs, mask):
            zs = [scores(hh, j) for hh in heads_r]
            pairs = [_log_sigmoid_pair(z) for z in zs]
            stays = [jnp.where(mask, p[1], 0.0) if mask is not None else p[1] for p in pairs]
            splits = [_split_bf16(s) for s in stays]
            afters = [_dot(tri, hi) + _dot(tri, lo) for hi, lo in splits]
            ws = [jnp.exp(pairs[hh][0] + afters[hh] + cs[hh]) for hh in heads_r]
            if mask is not None:
                ws = [jnp.where(mask, w, 0.0) for w in ws]
            pvs = [_dot(vt_ref[hh * nb + j], ws[hh].astype(BF16)) for hh in heads_r]
            return pvs, [jnp.sum(s, axis=0, keepdims=True) for s in stays]

        pvs, csums = sb_step(i, [0.0] * hps, key_i < qry_i)
        for hh in heads_r:
            st_ref[hh, 0:1, :] = csums[hh]
            acc_ref[hh] = pvs[hh]
        done_ref[0] = 0

        for j in reversed(range(nb - 1)):
            @pl.when((j < i) & (done_ref[0] == 0))
            def _(j=j):
                cs = [st_ref[hh, 0:1, :] for hh in heads_r]
                pvs, csums = sb_step(j, cs, None)
                c_max = None
                for hh in heads_r:
                    acc_ref[hh] = acc_ref[hh] + pvs[hh]
                    c_new = cs[hh] + csums[hh]
                    st_ref[hh, 0:1, :] = c_new
                    c_max = c_new if c_max is None else jnp.maximum(c_max, c_new)
                done_ref[0] = (jnp.max(c_max) < SB_DEAD).astype(jnp.int32)

        for hh in heads_r:
            o_ref[:, cols(hh)] = acc_ref[hh].T.astype(o_ref.dtype)


def _prompt_attn(q, k_buf, v_buf, layer, kind, batch, seq):
    d = q.shape[1]
    heads = d // HEAD_DIM
    blk = MOBA_BLOCK
    assert seq % blk == 0
    nb = seq // blk
    hps = 4 if heads % 4 == 0 else (2 if heads % 2 == 0 else 1)
    w = hps * HEAD_DIM
    scratch = [pltpu.VMEM((hps, seq, HEAD_DIM), F32), pltpu.VMEM((hps, seq, HEAD_DIM), F32),
               pltpu.SemaphoreType.DMA((2,)),
               pltpu.VMEM((hps * nb, blk, HEAD_DIM), BF16), pltpu.VMEM((hps * nb, HEAD_DIM, blk), BF16),
               pltpu.VMEM((hps, HEAD_DIM, blk), F32)]
    if kind == 0:
        nbp = -(-nb // SUBLANES) * SUBLANES
        scratch += [pltpu.VMEM((hps, nbp, HEAD_DIM), F32), pltpu.VMEM((hps, nbp, blk), F32)]
    scratch += [pltpu.VMEM((hps, SUBLANES, blk), F32), pltpu.SMEM((1,), jnp.int32)]
    return pl.pallas_call(
        functools.partial(_prompt_attn_kernel, kind=kind, nb=nb, hps=hps, layer=layer),
        grid=(batch, heads // hps, nb),
        in_specs=[
            pl.BlockSpec((blk, w), lambda b, h, i: (b * nb + i, h)),
            pl.BlockSpec(memory_space=pl.ANY),
            pl.BlockSpec(memory_space=pl.ANY),
        ],
        out_specs=pl.BlockSpec((blk, w), lambda b, h, i: (b * nb + i, h)),
        out_shape=jax.ShapeDtypeStruct((batch * seq, d), BF16),
        scratch_shapes=scratch,
        compiler_params=pltpu.CompilerParams(
            dimension_semantics=("parallel", "parallel", "arbitrary"), vmem_limit_bytes=VMEM_LIMIT),
        name="prompt_attn_moba" if kind == 0 else "prompt_attn_sb",
    )(q, k_buf, v_buf)


def _cache_means_kernel(pt_ref, *refs, bps, ppb, page, heads):
    o_ref = refs[bps * ppb]
    for j in range(bps):
        tot = jnp.zeros((heads, HEAD_DIM), F32)
        for u in range(ppb):
            tot = tot + jnp.sum(refs[j * ppb + u][...].reshape(page, heads, HEAD_DIM), axis=0)
        o_ref[j] = tot * (1.0 / (page * ppb))


def _cache_means(cache2, page_table, layer, page, heads):
    db, n_pages = page_table.shape
    ppb = MOBA_BLOCK // page
    nbp = n_pages // ppb
    bps = 4 if nbp % 4 == 0 else 1
    rows = page * heads

    def page_spec(j, u):
        return pl.BlockSpec((None, None, rows, HEAD_DIM),
                            lambda b, n, pt: (layer, pt[b, (n * bps + j) * ppb + u], 0, 0))

    return pl.pallas_call(
        functools.partial(_cache_means_kernel, bps=bps, ppb=ppb, page=page, heads=heads),
        grid_spec=pltpu.PrefetchScalarGridSpec(
            num_scalar_prefetch=1,
            grid=(db, nbp // bps),
            in_specs=[page_spec(j, u) for j in range(bps) for u in range(ppb)],
            out_specs=pl.BlockSpec((None, bps, heads, HEAD_DIM), lambda b, n, pt: (b, n, 0, 0)),
        ),
        out_shape=jax.ShapeDtypeStruct((db, nbp, heads, HEAD_DIM), F32),
        compiler_params=pltpu.CompilerParams(
            dimension_semantics=("parallel", "parallel"), vmem_limit_bytes=VMEM_LIMIT),
        name="cache_means",
    )(page_table, *([cache2] * (bps * ppb)))


def _block_diag_queries(q, heads, ts):
    rep = jnp.concatenate([q] * heads, axis=0) if heads > 1 else q
    r = lax.broadcasted_iota(jnp.int32, rep.shape, 0) // ts
    c = lax.broadcasted_iota(jnp.int32, rep.shape, 1) // HEAD_DIM
    return jnp.where(r == c, rep, 0.0)


def _moba_select_kernel(q_ref, means_ref, idx_ref, *, heads, ts, nbp):
    qbd = _block_diag_queries(q_ref[...], heads, ts)
    gate = _dot3(means_ref[...], qbd, NT_DIMS)
    rank, blk = _block_rank(gate, nbp, None)
    blk_f = blk.astype(F32)
    rows = [jnp.sum(jnp.where(rank == k, blk_f, 0.0), axis=0, keepdims=True) for k in range(MOBA_TOP_K)]
    rows.append(jnp.zeros((SUBLANES - MOBA_TOP_K, heads * ts), F32))
    idx_ref[...] = jnp.concatenate(rows, axis=0).astype(jnp.int32)


def _moba_select(q_s, means, heads):
    db, ts, d = q_s.shape
    nbp = means.shape[1]
    assert nbp >= MOBA_TOP_K
    return pl.pallas_call(
        functools.partial(_moba_select_kernel, heads=heads, ts=ts, nbp=nbp),
        grid=(db,),
        in_specs=[pl.BlockSpec((None, ts, d), lambda b: (b, 0, 0)),
                  pl.BlockSpec((None, nbp, d), lambda b: (b, 0, 0))],
        out_specs=pl.BlockSpec((None, SUBLANES, heads * ts), lambda b: (b, 0, 0)),
        out_shape=jax.ShapeDtypeStruct((db, SUBLANES, heads * ts), jnp.int32),
        compiler_params=pltpu.CompilerParams(
            dimension_semantics=("parallel",), vmem_limit_bytes=VMEM_LIMIT),
        name="moba_select",
    )(q_s, means)


def _sample_moba_kernel(idx_ref, pt_ref, q_ref, knew_ref, vnew_ref, ck_hbm, cv_hbm, o_ref,
                        kbuf, vbuf, sem, *, layer, heads, ts, page, ppb):
    b = pl.program_id(0)
    h = pl.program_id(1)
    g = b * heads + h
    total = pl.num_programs(0) * heads
    slot = g % 2
    ht = heads * ts
    scale = HEAD_DIM ** -0.5

    def block_copies(bb, hh, sl):
        cps = []
        for t in range(ts):
            for k in range(MOBA_TOP_K):
                blk = idx_ref[(bb * SUBLANES + k) * ht + hh * ts + t]
                f = t * MOBA_TOP_K + k
                for u in range(ppb):
                    pg = pt_ref[bb, blk * ppb + u]
                    rows = pl.ds(u * page, page)
                    cps.append(pltpu.make_async_copy(
                        ck_hbm.at[layer, pg, :, hh, :], kbuf.at[sl, f, rows], sem.at[0, sl]))
                    cps.append(pltpu.make_async_copy(
                        cv_hbm.at[layer, pg, :, hh, :], vbuf.at[sl, f, rows], sem.at[1, sl]))
        return cps

    @pl.when(g == 0)
    def _():
        for cp in block_copies(b, h, slot):
            cp.start()

    @pl.when(g + 1 < total)
    def _():
        g1 = g + 1
        for cp in block_copies(g1 // heads, g1 % heads, 1 - slot):
            cp.start()

    for cp in block_copies(b, h, slot):
        cp.wait()

    q = (q_ref[...] * scale).astype(BF16)
    row = lax.broadcasted_iota(jnp.int32, (BF16_ROWS, page), 0)
    key = lax.broadcasted_iota(jnp.int32, (BF16_ROWS, page), 1)
    row_o = lax.broadcasted_iota(jnp.int32, (BF16_ROWS, HEAD_DIM), 0)
    s_own = jnp.where(key <= row, _dot(q, knew_ref[...].astype(BF16), NT_DIMS), NEG)
    m_own = jnp.max(s_own, axis=1, keepdims=True)
    vn = vnew_ref[...].astype(BF16)
    nf = ts * MOBA_TOP_K
    ss = [_dot(q, kbuf[slot, f].astype(BF16), NT_DIMS) for f in range(nf)]
    out = jnp.zeros((BF16_ROWS, HEAD_DIM), F32)
    for t in range(ts):
        fs = [t * MOBA_TOP_K + k for k in range(MOBA_TOP_K)]
        m = m_own
        for f in fs:
            m = jnp.maximum(m, jnp.max(ss[f], axis=1, keepdims=True))
        p = jnp.exp(s_own - m)
        l = jnp.sum(p, axis=1, keepdims=True)
        acc = _dot(p.astype(BF16), vn)
        for f in fs:
            p = jnp.exp(ss[f] - m)
            l = l + jnp.sum(p, axis=1, keepdims=True)
            acc = acc + _dot(p.astype(BF16), vbuf[slot, f].astype(BF16))
        out = jnp.where(row_o == t, acc / l, out)
    o_ref[...] = out


def _sample_moba(idx, page_table, q16, knew, vnew, cache_k, cache_v, layer):
    db, heads = q16.shape[0], q16.shape[1]
    page = cache_k.shape[2]
    ts = idx.shape[0] // (db * SUBLANES * heads)
    ppb = MOBA_BLOCK // page
    nf = ts * MOBA_TOP_K
    per_bh = lambda rows: pl.BlockSpec((None, None, rows, HEAD_DIM), lambda b, h, idx, pt: (b, h, 0, 0))
    return pl.pallas_call(
        functools.partial(_sample_moba_kernel, layer=layer, heads=heads, ts=ts, page=page, ppb=ppb),
        grid_spec=pltpu.PrefetchScalarGridSpec(
            num_scalar_prefetch=2,
            grid=(db, heads),
            in_specs=[per_bh(BF16_ROWS), per_bh(page), per_bh(page),
                      pl.BlockSpec(memory_space=pl.ANY), pl.BlockSpec(memory_space=pl.ANY)],
            out_specs=per_bh(BF16_ROWS),
            scratch_shapes=[pltpu.VMEM((2, nf, MOBA_BLOCK, HEAD_DIM), F32),
                            pltpu.VMEM((2, nf, MOBA_BLOCK, HEAD_DIM), F32),
                            pltpu.SemaphoreType.DMA((2, 2))],
        ),
        out_shape=jax.ShapeDtypeStruct((db, heads, BF16_ROWS, HEAD_DIM), F32),
        compiler_params=pltpu.CompilerParams(
            dimension_semantics=("arbitrary", "arbitrary"), vmem_limit_bytes=VMEM_LIMIT),
        name="sample_attn_moba",
    )(idx, page_table, q16, knew, vnew, cache_k, cache_v)


def _sample_sb_kernel(pt_ref, q_ref, knew_ref, vnew_ref, ck_hbm, cv_hbm, o_ref,
                      kbuf, vbuf, sem, q16_ref, acc_ref, c_ref, flag_ref, *, layer, heads, ts, page,
                      n_pages, pp):
    b = pl.program_id(0)
    s = pl.program_id(1)
    n_steps = n_pages // pp
    g = b * n_steps + s
    total = pl.num_programs(0) * n_steps
    slot = g % 2
    ht = heads * ts
    grp = BF16_ROWS // ts
    scale = HEAD_DIM ** -0.5

    def page_copies(bb, ss, sl):
        cps = []
        for u in range(pp):
            pg = pt_ref[bb, n_pages - 1 - (ss * pp + u)]
            for h in range(heads):
                cps.append(pltpu.make_async_copy(
                    ck_hbm.at[layer, pg, :, h, :], kbuf.at[sl, u, h], sem.at[0, sl]))
                cps.append(pltpu.make_async_copy(
                    cv_hbm.at[layer, pg, :, h, :], vbuf.at[sl, u, h], sem.at[1, sl]))
        return cps

    @pl.when(g == 0)
    def _():
        for cp in page_copies(b, s, slot):
            cp.start()
        flag_ref[1 + slot] = 1

    @pl.when(s == 0)
    def _():
        flag_ref[0] = 0

    fetch_next = (g + 1 < total) & ((s + 1 >= n_steps) | (flag_ref[0] == 0))
    flag_ref[2 - slot] = fetch_next.astype(jnp.int32)

    @pl.when(fetch_next)
    def _():
        g1 = g + 1
        for cp in page_copies(g1 // n_steps, g1 % n_steps, 1 - slot):
            cp.start()

    row_t = lax.broadcasted_iota(jnp.int32, (ht, page), 0) % ts
    lane = lax.broadcasted_iota(jnp.int32, (ht, page), 1)
    grp_row = lax.broadcasted_iota(jnp.int32, (BF16_ROWS, page), 0) // ts
    tri = jnp.where(lax.broadcasted_iota(jnp.int32, (page, page), 0)
                    > lax.broadcasted_iota(jnp.int32, (page, page), 1), 1.0, 0.0).astype(BF16)

    def scores(k_at):
        parts = []
        for gi in range(heads // grp):
            acc = None
            for hh in range(grp):
                h = gi * grp + hh
                z = _dot(q16_ref[h], k_at(h).astype(BF16), NT_DIMS)
                acc = z if acc is None else acc + z
            parts.append(acc)
        return jnp.concatenate(parts, axis=0)

    def weighted_values(a, v_at):
        parts = []
        for gi in range(heads // grp):
            a16 = a[gi * BF16_ROWS:(gi + 1) * BF16_ROWS, :]
            acc = None
            for hh in range(grp):
                h = gi * grp + hh
                w = jnp.where(grp_row == hh, a16, 0.0).astype(BF16)
                o = _dot(w, v_at(h).astype(BF16))
                acc = o if acc is None else acc + o
            parts.append(acc)
        return jnp.concatenate(parts, axis=0)

    def process(pages, c, acc):
        pairs = [_log_sigmoid_pair(scores(k_at)) for k_at, _, _ in pages]
        stays = [jnp.where(m, p[1], 0.0) if m is not None else p[1] for p, (_, _, m) in zip(pairs, pages)]
        splits = [_split_bf16(s) for s in stays]
        afters = [_dot(hi, tri) + _dot(lo, tri) for hi, lo in splits]
        for u, (_, v_at, mask) in enumerate(pages):
            a = jnp.exp(pairs[u][0] + afters[u] + c)
            if mask is not None:
                a = jnp.where(mask, a, 0.0)
            acc = acc + weighted_values(a, v_at)
            c = c + jnp.sum(stays[u], axis=1, keepdims=True)
        return c, acc

    @pl.when(s == 0)
    def _():
        qbd = _block_diag_queries(q_ref[...], heads, ts) * scale
        for h in range(heads):
            gi = h // grp
            q16_ref[h] = qbd[gi * BF16_ROWS:(gi + 1) * BF16_ROWS,
                             h * HEAD_DIM:(h + 1) * HEAD_DIM].astype(BF16)
        new_keys = (lambda h: knew_ref[h], lambda h: vnew_ref[h], lane < row_t)
        c, acc = process([new_keys], jnp.zeros((ht, 1), F32), jnp.zeros((ht, HEAD_DIM), F32))
        c_ref[...] = c
        acc_ref[...] = acc

    have_pages = flag_ref[1 + slot] == 1

    @pl.when(have_pages)
    def _():
        for cp in page_copies(b, s, slot):
            cp.wait()

    @pl.when(have_pages & (flag_ref[0] == 0))
    def _():
        pages = [(lambda h, u=u: kbuf[slot, u, h], lambda h, u=u: vbuf[slot, u, h], None)
                 for u in range(pp)]
        c, acc = process(pages, c_ref[...], acc_ref[...])
        c_ref[...] = c
        acc_ref[...] = acc
        flag_ref[0] = (jnp.max(c) < SB_DEAD).astype(jnp.int32)

    @pl.when(s == n_steps - 1)
    def _():
        o_ref[...] = acc_ref[...]


def _sample_sb(q_s, knew, vnew, cache_k, cache_v, page_table, layer):
    db, ts, d = q_s.shape
    heads = d // HEAD_DIM
    page = cache_k.shape[2]
    n_pages = page_table.shape[1]
    pp = 4 if n_pages % 4 == 0 else 1
    ht = heads * ts
    assert BF16_ROWS % ts == 0 and heads % (BF16_ROWS // ts) == 0
    return pl.pallas_call(
        functools.partial(_sample_sb_kernel, layer=layer, heads=heads, ts=ts, page=page,
                          n_pages=n_pages, pp=pp),
        grid_spec=pltpu.PrefetchScalarGridSpec(
            num_scalar_prefetch=1,
            grid=(db, n_pages // pp),
            in_specs=[pl.BlockSpec((None, ts, d), lambda b, s, pt: (b, 0, 0)),
                      pl.BlockSpec((None, heads, page, HEAD_DIM), lambda b, s, pt: (b, 0, 0, 0)),
                      pl.BlockSpec((None, heads, page, HEAD_DIM), lambda b, s, pt: (b, 0, 0, 0)),
                      pl.BlockSpec(memory_space=pl.ANY), pl.BlockSpec(memory_space=pl.ANY)],
            out_specs=pl.BlockSpec((None, ht, HEAD_DIM), lambda b, s, pt: (b, 0, 0)),
            scratch_shapes=[pltpu.VMEM((2, pp, heads, page, HEAD_DIM), F32),
                            pltpu.VMEM((2, pp, heads, page, HEAD_DIM), F32),
                            pltpu.SemaphoreType.DMA((2, 2)),
                            pltpu.VMEM((heads, BF16_ROWS, HEAD_DIM), BF16),
                            pltpu.VMEM((ht, HEAD_DIM), F32),
                            pltpu.VMEM((ht, 1), F32),
                            pltpu.SMEM((3,), jnp.int32)],
        ),
        out_shape=jax.ShapeDtypeStruct((db, ht, HEAD_DIM), F32),
        compiler_params=pltpu.CompilerParams(
            dimension_semantics=("arbitrary", "arbitrary"), vmem_limit_bytes=VMEM_LIMIT),
        name="sample_attn_sb",
    )(page_table, q_s, knew, vnew, cache_k, cache_v)


def _route(logits, ng, epg):
    lane = lax.broadcasted_iota(jnp.int32, logits.shape, 1).astype(F32)
    ninf = -jnp.inf

    def top(v):
        m = jnp.max(v, axis=1, keepdims=True)
        idx = jnp.min(jnp.where(v == m, lane, float(LANES)), axis=1, keepdims=True)
        return m, idx

    gl = jnp.where(lane < ng, logits, ninf)
    gmax, gidx = top(gl)
    g_p = 1.0 / jnp.sum(jnp.exp(gl - gmax), axis=1, keepdims=True)
    lo = ng + gidx * epg
    el = jnp.where((lane >= lo) & (lane < lo + epg), logits, ninf)
    m1, i1 = top(el)
    m2, i2 = top(jnp.where(lane == i1, ninf, el))
    t = jnp.exp(m2 - m1)
    w0 = g_p / (1.0 + t)
    w1 = g_p * t / (1.0 + t)
    return jnp.where(lane == 0, i1 - ng,
                     jnp.where(lane == 1, i2 - ng,
                               jnp.where(lane == 2, w0, jnp.where(lane == 3, w1, 0.0))))


def _proj_ln_kernel(o_ref, w_ref, x_ref, g_ref, b_ref, wr_ref, br_ref, y_ref, r_ref, *, alpha, ng, epg):
    h = alpha * x_ref[...] + _dot(o_ref[...], w_ref[...])
    y = _layernorm(h, g_ref[...], b_ref[...])
    y_ref[...] = y
    r_ref[...] = _route(_dot3(y, wr_ref[...], NN_DIMS) + br_ref[...], ng, epg)


def _proj_ln(o, w_bf, x, g, b, w_r, b_r, layer, alpha, ng, epg):
    nt, d = x.shape
    tm = ROW_TILE
    return pl.pallas_call(
        functools.partial(_proj_ln_kernel, alpha=alpha, ng=ng, epg=epg),
        grid=(nt // tm,),
        in_specs=[
            pl.BlockSpec((tm, d), lambda i: (i, 0)),
            pl.BlockSpec((None, d, d), lambda i: (layer, 0, 0)),
            pl.BlockSpec((tm, d), lambda i: (i, 0)),
            pl.BlockSpec((None, 1, d), lambda i: (layer, 0, 0)),
            pl.BlockSpec((None, 1, d), lambda i: (layer, 0, 0)),
            pl.BlockSpec((None, d, LANES), lambda i: (layer, 0, 0)),
            pl.BlockSpec((None, 1, LANES), lambda i: (layer, 0, 0)),
        ],
        out_specs=[pl.BlockSpec((tm, d), lambda i: (i, 0)), pl.BlockSpec((tm, LANES), lambda i: (i, 0))],
        out_shape=[jax.ShapeDtypeStruct((nt, d), F32), jax.ShapeDtypeStruct((nt, LANES), F32)],
        compiler_params=pltpu.CompilerParams(
            dimension_semantics=("parallel",), vmem_limit_bytes=VMEM_LIMIT),
        name="proj_ln",
    )(o, w_bf, x, g, b, w_r, b_r)


def _route_plan(e_idx, ne, tm):
    n2 = e_idx.size
    flat = e_idx.reshape(-1)
    onehot = (flat[:, None] == jnp.arange(ne, dtype=jnp.int32)[None, :]).astype(jnp.int32)
    csum = jnp.cumsum(onehot, axis=0)
    counts = csum[-1]
    pos_in_e = jnp.sum((csum - onehot) * onehot, axis=1)
    tiles_e = (counts + tm - 1) // tm
    tile_end = jnp.cumsum(tiles_e)
    tile_start = tile_end - tiles_e
    dest = jnp.sum(onehot * tile_start[None, :], axis=1) * tm + pos_in_e
    n_tiles = -(-n2 // tm) + ne
    gidx = jnp.arange(n_tiles, dtype=jnp.int32)
    active = gidx < tile_end[-1]
    te = jnp.sum((gidx[:, None] >= tile_end[None, :]).astype(jnp.int32), axis=1)
    last_e = jnp.max(jnp.where(counts > 0, jnp.arange(ne, dtype=jnp.int32), 0))
    te = jnp.where(active, te, last_e).astype(jnp.int32)
    tok = jnp.zeros((n_tiles * tm,), jnp.int32).at[dest].set(jnp.arange(n2, dtype=jnp.int32) // 2)
    return dest.astype(jnp.int32), tok, te, active.astype(jnp.int32), n_tiles


def _gather_rows_start(idx_ref, base, n, src_hbm, dst, sem):
    def start(r, carry):
        pltpu.make_async_copy(src_hbm.at[pl.ds(idx_ref[base + r], 1)], dst.at[pl.ds(r, 1)], sem).start()
        return carry

    lax.fori_loop(0, n, start, 0, unroll=8)


def _gather_rows_wait(n, src_hbm, dst, sem):
    def wait(r, carry):
        pltpu.make_async_copy(src_hbm.at[pl.ds(0, 1)], dst.at[pl.ds(r, 1)], sem).wait()
        return carry

    lax.fori_loop(0, n, wait, 0, unroll=8)


def _expert_kernel(te_ref, act_ref, tok_ref, x_hbm, wg_ref, wu_ref, wd_ref, y_ref,
                   xbuf, wgb, wub, wdb, sem, *, tm):
    g = pl.program_id(0)
    n_tiles = pl.num_programs(0)
    slot = g % 2
    nxt = jnp.minimum(g + 1, n_tiles - 1)

    @pl.when((g == 0) & (act_ref[0] == 1))
    def _():
        _gather_rows_start(tok_ref, 0, tm, x_hbm, xbuf.at[0], sem.at[0])

    @pl.when((g + 1 < n_tiles) & (act_ref[nxt] == 1))
    def _():
        _gather_rows_start(tok_ref, nxt * tm, tm, x_hbm, xbuf.at[1 - slot], sem.at[1 - slot])

    prev = te_ref[jnp.maximum(g - 1, 0)]

    @pl.when((g == 0) | (te_ref[g] != prev))
    def _():
        wgb[...] = wg_ref[...].astype(BF16)
        wub[...] = wu_ref[...].astype(BF16)
        wdb[...] = wd_ref[...].astype(BF16)

    @pl.when(act_ref[g] == 1)
    def _():
        _gather_rows_wait(tm, x_hbm, xbuf.at[slot], sem.at[slot])
        xb = xbuf[slot].astype(BF16)
        gate = _dot(xb, wgb[...])
        up = _dot(xb, wub[...])
        h = gate * jax.nn.sigmoid(gate) * up
        y_ref[...] = _dot(h.astype(BF16), wdb[...])

    @pl.when(act_ref[g] == 0)
    def _():
        y_ref[...] = jnp.zeros(y_ref.shape, F32)


def _expert_mlp(x, tok, te, act, n_tiles, w_gate, w_up, w_down, layer, tm):
    nt, d = x.shape
    f = w_gate.shape[-1]
    return pl.pallas_call(
        functools.partial(_expert_kernel, tm=tm),
        grid_spec=pltpu.PrefetchScalarGridSpec(
            num_scalar_prefetch=3,
            grid=(n_tiles,),
            in_specs=[
                pl.BlockSpec(memory_space=pl.ANY),
                pl.BlockSpec((None, None, d, f), lambda g, te, act, tok: (layer, te[g], 0, 0)),
                pl.BlockSpec((None, None, d, f), lambda g, te, act, tok: (layer, te[g], 0, 0)),
                pl.BlockSpec((None, None, f, d), lambda g, te, act, tok: (layer, te[g], 0, 0)),
            ],
            out_specs=pl.BlockSpec((tm, d), lambda g, te, act, tok: (g, 0)),
            scratch_shapes=[
                pltpu.VMEM((2, tm, d), F32),
                pltpu.VMEM((d, f), BF16), pltpu.VMEM((d, f), BF16), pltpu.VMEM((f, d), BF16),
                pltpu.SemaphoreType.DMA((2,)),
            ],
        ),
        out_shape=jax.ShapeDtypeStruct((n_tiles * tm, d), F32),
        compiler_params=pltpu.CompilerParams(
            dimension_semantics=("arbitrary",), vmem_limit_bytes=VMEM_LIMIT),
        name="expert_mlp",
    )(te, act, tok, x, w_gate, w_up, w_down)


def _combine_ln_kernel(dest_ref, r_ref, x_ref, g_ref, b_ref, y_hbm, o_ref, ob_ref, ybuf, sem,
                       *, tm, alpha):
    i = pl.program_id(0)
    n_tiles = pl.num_programs(0)
    slot = i % 2
    rows = tm * TOP_K_EXPERTS

    @pl.when(i == 0)
    def _():
        _gather_rows_start(dest_ref, 0, rows, y_hbm, ybuf.at[0], sem.at[0])

    @pl.when(i + 1 < n_tiles)
    def _():
        _gather_rows_start(dest_ref, (i + 1) * rows, rows, y_hbm, ybuf.at[1 - slot], sem.at[1 - slot])

    _gather_rows_wait(rows, y_hbm, ybuf.at[slot], sem.at[slot])
    y0 = ybuf[slot, 0:tm, :]
    y1 = ybuf[slot, tm:2 * tm, :]
    r = r_ref[...]
    moe = r[:, 2:3] * y0 + r[:, 3:4] * y1
    y = _layernorm(alpha * x_ref[...] + moe, g_ref[...], b_ref[...])
    o_ref[...] = y
    ob_ref[...] = y.astype(BF16)


def _combine_ln(dest, rinfo, x, g, b, y_sorted, layer, alpha):
    nt, d = x.shape
    tm = ROW_TILE
    dest = dest.reshape(nt // tm, tm, TOP_K_EXPERTS).transpose(0, 2, 1).reshape(-1)
    return pl.pallas_call(
        functools.partial(_combine_ln_kernel, tm=tm, alpha=alpha),
        grid_spec=pltpu.PrefetchScalarGridSpec(
            num_scalar_prefetch=1,
            grid=(nt // tm,),
            in_specs=[
                pl.BlockSpec((tm, LANES), lambda i, dest: (i, 0)),
                pl.BlockSpec((tm, d), lambda i, dest: (i, 0)),
                pl.BlockSpec((None, 1, d), lambda i, dest: (layer, 0, 0)),
                pl.BlockSpec((None, 1, d), lambda i, dest: (layer, 0, 0)),
                pl.BlockSpec(memory_space=pl.ANY),
            ],
            out_specs=[pl.BlockSpec((tm, d), lambda i, dest: (i, 0)),
                       pl.BlockSpec((tm, d), lambda i, dest: (i, 0))],
            scratch_shapes=[pltpu.VMEM((2, tm * TOP_K_EXPERTS, d), F32), pltpu.SemaphoreType.DMA((2,))],
        ),
        out_shape=[jax.ShapeDtypeStruct((nt, d), F32), jax.ShapeDtypeStruct((nt, d), BF16)],
        compiler_params=pltpu.CompilerParams(
            dimension_semantics=("arbitrary",), vmem_limit_bytes=VMEM_LIMIT),
        name="combine_ln",
    )(dest, rinfo, x, g, b, y_sorted)


def _rope_tables(pos):
    half = HEAD_DIM // 2
    inv_freq = ROPE_THETA ** (-jnp.arange(half, dtype=F32) / half)
    ang = pos.astype(F32)[:, None] * inv_freq[None, :]
    cos, sin = jnp.cos(ang), jnp.sin(ang)
    return jnp.concatenate([cos, cos], axis=-1), jnp.concatenate([-sin, sin], axis=-1)


def kernel(x_prompt, x_sample, cache_k, cache_v, page_table, w_qkv, w_o, ln_mix_g, ln_mix_b,
           w_group, b_group, w_route, b_route, w_gate, w_up, w_down, ln_ffn_g, ln_ffn_b):
    batch, seq, d = x_prompt.shape
    db, ts, _ = x_sample.shape
    depth = w_qkv.shape[0]
    heads = d // HEAD_DIM
    n_pool, page = cache_k.shape[1], cache_k.shape[2]
    n_pages = page_table.shape[1]
    past = n_pages * page
    ng, ne = w_group.shape[-1], w_route.shape[-1]
    epg = ne // ng
    assert MOBA_BLOCK % page == 0 and past % MOBA_BLOCK == 0 and ts <= page and ts <= BF16_ROWS
    assert ng + ne <= LANES
    alpha = (2 * depth) ** 0.25

    n_p, n_s = batch * seq, db * ts
    nt = n_p + -(-n_s // ROW_TILE) * ROW_TILE
    pad = nt - n_p - n_s

    pos = jnp.concatenate([jnp.tile(jnp.arange(seq, dtype=jnp.int32), batch),
                           jnp.tile(past + jnp.arange(ts, dtype=jnp.int32), db),
                           jnp.zeros((pad,), jnp.int32)])
    cos, sin = _rope_tables(pos)

    x = jnp.concatenate([x_prompt.reshape(n_p, d), x_sample.reshape(n_s, d),
                         jnp.zeros((pad, d), F32)], axis=0)
    xb = x.astype(BF16)
    wqkv_bf = w_qkv.astype(BF16)
    wo_bf = w_o.astype(BF16)
    w_r = jnp.concatenate([w_group, w_route, jnp.zeros((depth, d, LANES - ng - ne), F32)], axis=-1)
    b_r = jnp.concatenate([b_group, b_route, jnp.zeros((depth, LANES - ng - ne), F32)],
                          axis=-1).reshape(depth, 1, LANES)
    g_mix, b_mix = ln_mix_g.reshape(depth, 1, d), ln_mix_b.reshape(depth, 1, d)
    g_ffn, b_ffn = ln_ffn_g.reshape(depth, 1, d), ln_ffn_b.reshape(depth, 1, d)
    cache_k2 = cache_k.reshape(depth, n_pool, page * heads, HEAD_DIM)

    def per_head(a, rows):
        a = a.reshape(db, ts, heads, HEAD_DIM).transpose(0, 2, 1, 3)
        return jnp.pad(a, ((0, 0), (0, 0), (0, rows - ts), (0, 0)))

    kp = jnp.zeros((depth, n_p, heads, HEAD_DIM), F32)
    vp = jnp.zeros((depth, n_p, heads, HEAD_DIM), F32)
    ks, vs = [], []
    for layer in range(depth):
        kind = layer % 2
        rope = kind == 0
        q_p = _head_proj(xb, wqkv_bf, cos, sin, layer, 0, rope, n_p)
        kp = _head_proj(xb, wqkv_bf, cos, sin, layer, 1, rope, n_p, kp)
        vp = _head_proj(xb, wqkv_bf, cos, sin, layer, 2, False, n_p, vp)
        o_p = _prompt_attn(q_p, kp, vp, layer, kind, batch, seq)
        qkv_s = _qkv_proj(xb[n_p:], wqkv_bf, cos[n_p:], sin[n_p:], layer, rope=rope)
        q_s, k_s, v_s = (qkv_s[c, :n_s] for c in range(3))
        knew, vnew = per_head(k_s, page), per_head(v_s, page)
        if kind == 0:
            means = _cache_means(cache_k2, page_table, layer, page, heads)
            idx = _moba_select(q_s.reshape(db, ts, d), means.reshape(db, means.shape[1], d), heads)
            o_s = _sample_moba(idx.reshape(-1), page_table, per_head(q_s, BF16_ROWS), knew, vnew,
                               cache_k, cache_v, layer)
            o_s = o_s[:, :, :ts]
        else:
            o_s = _sample_sb(q_s.reshape(db, ts, d), knew, vnew, cache_k, cache_v, page_table, layer)
            o_s = o_s.reshape(db, heads, ts, HEAD_DIM)
        o_s = o_s.transpose(0, 2, 1, 3).reshape(n_s, d)
        o = jnp.concatenate([o_p, o_s.astype(BF16), jnp.zeros((pad, d), BF16)], axis=0)
        x1, rinfo = _proj_ln(o, wo_bf, x, g_mix, b_mix, w_r, b_r, layer, alpha, ng, epg)

        e_idx = rinfo[:, :TOP_K_EXPERTS].astype(jnp.int32)
        dest, tok, te, act, n_tiles = _route_plan(e_idx, ne, ROW_TILE)
        y_sorted = _expert_mlp(x1, tok, te, act, n_tiles, w_gate, w_up, w_down, layer, ROW_TILE)
        x, xb = _combine_ln(dest, rinfo, x1, g_ffn, b_ffn, y_sorted, layer, alpha)

        ks.append(k_s.reshape(db, ts, heads, HEAD_DIM))
        vs.append(v_s.reshape(db, ts, heads, HEAD_DIM))

    return (x[:n_p].reshape(batch, seq, d), x[n_p:n_p + n_s].reshape(db, ts, d),
            kp.reshape(depth, batch, seq, heads, HEAD_DIM), vp.reshape(depth, batch, seq, heads, HEAD_DIM),
            jnp.stack(ks), jnp.stack(vs))
```

```python
import functools

import jax
import jax.numpy as jnp
from jax import lax
from jax.experimental import pallas as pl
from jax.experimental.pallas import tpu as pltpu

HEAD_DIM = 128
MOBA_BLOCK = 256
MOBA_TOP_K = 3
ROPE_THETA = 10000.0
LN_EPS = 1e-5
TOP_K_EXPERTS = 2
LANES = 128
SUBLANES = 8
BF16_ROWS = 16
ROW_TILE = 256
NEG = -1e30
SB_DEAD = -105.0
VMEM_LIMIT = 48 * 1024 * 1024

F32 = jnp.float32
BF16 = jnp.bfloat16
NT_DIMS = (((1,), (1,)), ((), ()))
NN_DIMS = (((1,), (0,)), ((), ()))


def _split_bf16(a):
    hi = a.astype(BF16)
    lo = (a - hi.astype(F32)).astype(BF16)
    return hi, lo


def _dot3(a, b, dims):
    ah, al = _split_bf16(a)
    bh, bl = _split_bf16(b)
    d = lambda x, y: lax.dot_general(x, y, dims, preferred_element_type=F32)
    return d(ah, bh) + d(ah, bl) + d(al, bh)


def _dot(a, b, dims=NN_DIMS):
    return lax.dot_general(a, b, dims, preferred_element_type=F32)


def _pick_tile(n, target, quantum):
    best = quantum
    t = quantum
    while t <= target:
        if n % t == 0:
            best = t
        t += quantum
    assert n % best == 0, (n, quantum)
    return best


def _layernorm(h, g, b):
    mu = jnp.mean(h, axis=-1, keepdims=True)
    d = h - mu
    var = jnp.mean(d * d, axis=-1, keepdims=True)
    return d * lax.rsqrt(var + LN_EPS) * g + b


def _log_sigmoid_pair(z):
    sp = jnp.log(1.0 + jnp.exp(-jnp.abs(z)))
    pos = jnp.minimum(z, 0.0) - sp
    return pos, pos - z


def _block_rank(g, n_blocks, n_valid):
    blk = lax.broadcasted_iota(jnp.int32, g.shape, 0)
    rank = jnp.zeros(g.shape, jnp.int32)
    for n in range(n_blocks):
        gn = g[n:n + 1, :]
        beats = (gn > g) | ((gn == g) & (blk > n))
        inc = jnp.where(beats, 1, 0)
        if n_valid is not None:
            inc = inc * (n < n_valid).astype(jnp.int32)
        rank = rank + inc
    return rank, blk


def _topk_bias(g, n_blocks, n_valid):
    rank, blk = _block_rank(g, n_blocks, n_valid)
    sel = (rank < MOBA_TOP_K) & (blk < n_valid)
    return jnp.where(sel, 0.0, NEG)


def _qkv_kernel(x_ref, w_ref, cos_ref, sin_ref, o_ref, *, rope_tiles):
    acc = _dot(x_ref[...], w_ref[...].astype(BF16))
    if rope_tiles == 0:
        o_ref[...] = acc
        return
    j = pl.program_id(0)

    @pl.when(j < rope_tiles)
    def _():
        cos = cos_ref[...]
        sin = sin_ref[...]
        for c in range(acc.shape[1] // HEAD_DIM):
            blk = acc[:, c * HEAD_DIM:(c + 1) * HEAD_DIM]
            rot = pltpu.roll(blk, HEAD_DIM // 2, axis=1)
            o_ref[:, c * HEAD_DIM:(c + 1) * HEAD_DIM] = blk * cos + rot * sin

    @pl.when(j >= rope_tiles)
    def _():
        o_ref[...] = acc


def _qkv_proj(xb, w_bf, cos, sin, layer, rope):
    nt, d = xb.shape
    tm = _pick_tile(nt, 768, ROW_TILE)
    tn = min(512, d)
    ncol = d // tn
    return pl.pallas_call(
        functools.partial(_qkv_kernel, rope_tiles=2 * ncol if rope else 0),
        grid=(3 * ncol, nt // tm),
        in_specs=[
            pl.BlockSpec((tm, d), lambda j, i: (i, 0)),
            pl.BlockSpec((None, d, tn), lambda j, i: (layer, 0, j)),
            pl.BlockSpec((tm, HEAD_DIM), lambda j, i: (i, 0)),
            pl.BlockSpec((tm, HEAD_DIM), lambda j, i: (i, 0)),
        ],
        out_specs=pl.BlockSpec((None, tm, tn), lambda j, i: (j // ncol, i, j % ncol)),
        out_shape=jax.ShapeDtypeStruct((3, nt, d), F32),
        compiler_params=pltpu.CompilerParams(
            dimension_semantics=("parallel", "parallel"), vmem_limit_bytes=VMEM_LIMIT),
        name="qkv_proj",
    )(xb, w_bf, cos, sin)


def _head_proj_kernel(x_ref, w_ref, cos_ref, sin_ref, *rest, rope, per_head):
    o_ref, wb_ref = rest[-2], rest[-1]

    @pl.when(pl.program_id(1) == 0)
    def _():
        wb_ref[...] = w_ref[...].astype(BF16)

    acc = _dot(x_ref[...], wb_ref[...])
    for c in range(acc.shape[1] // HEAD_DIM):
        blk = acc[:, c * HEAD_DIM:(c + 1) * HEAD_DIM]
        if rope:
            blk = blk * cos_ref[...] + pltpu.roll(blk, HEAD_DIM // 2, axis=1) * sin_ref[...]
        if per_head:
            o_ref[:, c, :] = blk
        else:
            o_ref[:, c * HEAD_DIM:(c + 1) * HEAD_DIM] = blk


def _head_proj(xb, w_bf, cos, sin, layer, part, rope, n_rows, out_buf=None):
    d = xb.shape[1]
    heads = d // HEAD_DIM
    nh = SUBLANES if heads % SUBLANES == 0 else heads
    tn = nh * HEAD_DIM
    ncol = d // tn
    tm = _pick_tile(n_rows, 1024, ROW_TILE)
    in_specs = [
        pl.BlockSpec((tm, d), lambda j, i: (i, 0)),
        pl.BlockSpec((None, d, tn), lambda j, i: (layer, 0, part * ncol + j)),
        pl.BlockSpec((tm, HEAD_DIM), lambda j, i: (i, 0)),
        pl.BlockSpec((tm, HEAD_DIM), lambda j, i: (i, 0)),
    ]
    args = [xb, w_bf, cos, sin]
    if out_buf is None:
        out_specs = pl.BlockSpec((tm, tn), lambda j, i: (i, j))
        out_shape = jax.ShapeDtypeStruct((n_rows, d), F32)
        aliases = {}
    else:
        in_specs.append(pl.BlockSpec(memory_space=pl.ANY))
        args.append(out_buf)
        out_specs = pl.BlockSpec((None, tm, nh, HEAD_DIM), lambda j, i: (layer, i, j, 0))
        out_shape = jax.ShapeDtypeStruct(out_buf.shape, F32)
        aliases = {4: 0}
    return pl.pallas_call(
        functools.partial(_head_proj_kernel, rope=rope, per_head=out_buf is not None),
        grid=(ncol, n_rows // tm),
        in_specs=in_specs,
        out_specs=out_specs,
        out_shape=out_shape,
        input_output_aliases=aliases,
        scratch_shapes=[pltpu.VMEM((d, tn), BF16)],
        compiler_params=pltpu.CompilerParams(
            dimension_semantics=("parallel", "arbitrary"), vmem_limit_bytes=VMEM_LIMIT),
        name="head_proj_" + "qkv"[part],
    )(*args)


def _prompt_attn_kernel(q_ref, k_hbm, v_hbm, o_ref, kst_ref, vst_ref, sem, kb_ref, vt_ref, acc_ref,
                        *scratch, kind, nb, hps, layer):
    blk = MOBA_BLOCK
    seq = nb * blk
    b = pl.program_id(0)
    hg = pl.program_id(1)
    i = pl.program_id(2)
    scale = HEAD_DIM ** -0.5
    cols = lambda hh: slice(hh * HEAD_DIM, (hh + 1) * HEAD_DIM)

    @pl.when(i == 0)
    def _():
        rows = pl.ds(pl.multiple_of(b * seq, blk), seq)
        k_cps = [pltpu.make_async_copy(k_hbm.at[layer, rows, hg * hps + hh, :], kst_ref.at[hh], sem.at[0])
                 for hh in range(hps)]
        v_cps = [pltpu.make_async_copy(v_hbm.at[layer, rows, hg * hps + hh, :], vst_ref.at[hh], sem.at[1])
                 for hh in range(hps)]
        for cp in k_cps + v_cps:
            cp.start()
        if kind == 0:
            scratch[0][...] = jnp.zeros(scratch[0].shape, F32)
        for cp in k_cps:
            cp.wait()
        for hh in range(hps):
            for j in range(nb):
                kj = kst_ref[hh, j * blk:(j + 1) * blk, :]
                kb_ref[hh * nb + j] = kj.astype(BF16)
                if kind == 0:
                    scratch[0][hh, j:j + 1, :] = jnp.sum(kj, axis=0, keepdims=True) * (1.0 / blk)
        for cp in v_cps:
            cp.wait()
        for hh in range(hps):
            for j in range(nb):
                vt_ref[hh * nb + j] = vst_ref[hh, j * blk:(j + 1) * blk, :].T.astype(BF16)

    q = [q_ref[:, cols(hh)] for hh in range(hps)]
    qs = [(x * scale).astype(BF16) for x in q]
    key_i = lax.broadcasted_iota(jnp.int32, (blk, blk), 0)
    qry_i = lax.broadcasted_iota(jnp.int32, (blk, blk), 1)

    def scores(hh, j):
        return _dot(kb_ref[hh * nb + j], qs[hh], NT_DIMS)

    st_ref, done_ref = scratch[-2], scratch[-1]
    heads_r = range(hps)
    if kind == 0:
        means_ref, bias_ref = scratch[0], scratch[1]
        for hh in heads_r:
            gate = _dot3(means_ref[hh], q[hh], NT_DIMS)
            bias_ref[hh] = _topk_bias(gate, nb, i)
        ss = [jnp.where(key_i <= qry_i, scores(hh, i), NEG) for hh in heads_r]
        ms = [jnp.max(s, axis=0, keepdims=True) for s in ss]
        ps = [jnp.exp(s - m) for s, m in zip(ss, ms)]
        pvs = [_dot(vt_ref[hh * nb + i], ps[hh].astype(BF16)) for hh in heads_r]
        for hh in heads_r:
            st_ref[hh, 0:1, :] = ms[hh]
            st_ref[hh, 1:2, :] = jnp.sum(ps[hh], axis=0, keepdims=True)
            acc_ref[hh] = pvs[hh]

        for j in range(nb - 1):
            @pl.when(j < i)
            def _(j=j):
                ss = [scores(hh, j) + bias_ref[hh, j:j + 1, :] for hh in heads_r]
                m_old = [st_ref[hh, 0:1, :] for hh in heads_r]
                m_new = [jnp.maximum(m_old[hh], jnp.max(ss[hh], axis=0, keepdims=True)) for hh in heads_r]
                ps = [jnp.exp(ss[hh] - m_new[hh]) for hh in heads_r]
                pvs = [_dot(vt_ref[hh * nb + j], ps[hh].astype(BF16)) for hh in heads_r]
                for hh in heads_r:
                    a = jnp.exp(m_old[hh] - m_new[hh])
                    st_ref[hh, 0:1, :] = m_new[hh]
                    st_ref[hh, 1:2, :] = a * st_ref[hh, 1:2, :] + jnp.sum(ps[hh], axis=0, keepdims=True)
                    acc_ref[hh] = a * acc_ref[hh] + pvs[hh]

        for hh in heads_r:
            o_ref[:, cols(hh)] = (acc_ref[hh] / st_ref[hh, 1:2, :]).T.astype(o_ref.dtype)
    else:
        tri = jnp.where(qry_i > key_i, 1.0, 0.0).astype(BF16)

        def sb_step(j, cs, mask):
            zs = [scores(hh, j) for hh in heads_r]
            pairs = [_log_sigmoid_pair(z) for z in zs]
            stays = [jnp.where(mask, p[1], 0.0) if mask is not None else p[1] for p in pairs]
            splits = [_split_bf16(s) for s in stays]
            afters = [_dot(tri, hi) + _dot(tri, lo) for hi, lo in splits]
            ws = [jnp.exp(pairs[hh][0] + afters[hh] + cs[hh]) for hh in heads_r]
            if mask is not None:
                ws = [jnp.where(mask, w, 0.0) for w in ws]
            pvs = [_dot(vt_ref[hh * nb + j], ws[hh].astype(BF16)) for hh in heads_r]
            return pvs, [jnp.sum(s, axis=0, keepdims=True) for s in stays]

        pvs, csums = sb_step(i, [0.0] * hps, key_i < qry_i)
        for hh in heads_r:
            st_ref[hh, 0:1, :] = csums[hh]
            acc_ref[hh] = pvs[hh]
        done_ref[0] = 0

        for j in reversed(range(nb - 1)):
            @pl.when((j < i) & (done_ref[0] == 0))
            def _(j=j):
                cs = [st_ref[hh, 0:1, :] for hh in heads_r]
                pvs, csums = sb_step(j, cs, None)
                c_max = None
                for hh in heads_r:
                    acc_ref[hh] = acc_ref[hh] + pvs[hh]
                    c_new = cs[hh] + csums[hh]
                    st_ref[hh, 0:1, :] = c_new
                    c_max = c_new if c_max is None else jnp.maximum(c_max, c_new)
                done_ref[0] = (jnp.max(c_max) < SB_DEAD).astype(jnp.int32)

        for hh in heads_r:
            o_ref[:, cols(hh)] = acc_ref[hh].T.astype(o_ref.dtype)


def _prompt_attn(q, k_buf, v_buf, layer, kind, batch, seq):
    d = q.shape[1]
    heads = d // HEAD_DIM
    blk = MOBA_BLOCK
    assert seq % blk == 0
    nb = seq // blk
    hps = next(n for n in (8, 4, 2, 1) if heads % n == 0)
    w = hps * HEAD_DIM
    scratch = [pltpu.VMEM((hps, seq, HEAD_DIM), F32), pltpu.VMEM((hps, seq, HEAD_DIM), F32),
               pltpu.SemaphoreType.DMA((2,)),
               pltpu.VMEM((hps * nb, blk, HEAD_DIM), BF16), pltpu.VMEM((hps * nb, HEAD_DIM, blk), BF16),
               pltpu.VMEM((hps, HEAD_DIM, blk), F32)]
    if kind == 0:
        nbp = -(-nb // SUBLANES) * SUBLANES
        scratch += [pltpu.VMEM((hps, nbp, HEAD_DIM), F32), pltpu.VMEM((hps, nbp, blk), F32)]
    scratch += [pltpu.VMEM((hps, SUBLANES, blk), F32), pltpu.SMEM((1,), jnp.int32)]
    return pl.pallas_call(
        functools.partial(_prompt_attn_kernel, kind=kind, nb=nb, hps=hps, layer=layer),
        grid=(batch, heads // hps, nb),
        in_specs=[
            pl.BlockSpec((blk, w), lambda b, h, i: (b * nb + i, h)),
            pl.BlockSpec(memory_space=pl.ANY),
            pl.BlockSpec(memory_space=pl.ANY),
        ],
        out_specs=pl.BlockSpec((blk, w), lambda b, h, i: (b * nb + i, h)),
        out_shape=jax.ShapeDtypeStruct((batch * seq, d), BF16),
        scratch_shapes=scratch,
        compiler_params=pltpu.CompilerParams(
            dimension_semantics=("parallel", "parallel", "arbitrary"), vmem_limit_bytes=VMEM_LIMIT),
        name="prompt_attn_moba" if kind == 0 else "prompt_attn_sb",
    )(q, k_buf, v_buf)


def _cache_means_kernel(pt_ref, *refs, bps, ppb, page, heads):
    o_ref = refs[bps * ppb]
    for j in range(bps):
        tot = jnp.zeros((heads, HEAD_DIM), F32)
        for u in range(ppb):
            tot = tot + jnp.sum(refs[j * ppb + u][...].reshape(page, heads, HEAD_DIM), axis=0)
        o_ref[j] = tot * (1.0 / (page * ppb))


def _cache_means(cache2, page_table, layer, page, heads):
    db, n_pages = page_table.shape
    ppb = MOBA_BLOCK // page
    nbp = n_pages // ppb
    bps = 4 if nbp % 4 == 0 else 1
    rows = page * heads

    def page_spec(j, u):
        return pl.BlockSpec((None, None, rows, HEAD_DIM),
                            lambda b, n, pt: (layer, pt[b, (n * bps + j) * ppb + u], 0, 0))

    return pl.pallas_call(
        functools.partial(_cache_means_kernel, bps=bps, ppb=ppb, page=page, heads=heads),
        grid_spec=pltpu.PrefetchScalarGridSpec(
            num_scalar_prefetch=1,
            grid=(db, nbp // bps),
            in_specs=[page_spec(j, u) for j in range(bps) for u in range(ppb)],
            out_specs=pl.BlockSpec((None, bps, heads, HEAD_DIM), lambda b, n, pt: (b, n, 0, 0)),
        ),
        out_shape=jax.ShapeDtypeStruct((db, nbp, heads, HEAD_DIM), F32),
        compiler_params=pltpu.CompilerParams(
            dimension_semantics=("parallel", "parallel"), vmem_limit_bytes=VMEM_LIMIT),
        name="cache_means",
    )(page_table, *([cache2] * (bps * ppb)))


def _block_diag_queries(q, heads, ts):
    rep = jnp.concatenate([q] * heads, axis=0) if heads > 1 else q
    r = lax.broadcasted_iota(jnp.int32, rep.shape, 0) // ts
    c = lax.broadcasted_iota(jnp.int32, rep.shape, 1) // HEAD_DIM
    return jnp.where(r == c, rep, 0.0)


def _moba_select_kernel(q_ref, means_ref, idx_ref, *, heads, ts, nbp):
    qbd = _block_diag_queries(q_ref[...], heads, ts)
    gate = _dot3(means_ref[...], qbd, NT_DIMS)
    rank, blk = _block_rank(gate, nbp, None)
    blk_f = blk.astype(F32)
    rows = [jnp.sum(jnp.where(rank == k, blk_f, 0.0), axis=0, keepdims=True) for k in range(MOBA_TOP_K)]
    rows.append(jnp.zeros((SUBLANES - MOBA_TOP_K, heads * ts), F32))
    idx_ref[...] = jnp.concatenate(rows, axis=0).astype(jnp.int32)


def _moba_select(q_s, means, heads):
    db, ts, d = q_s.shape
    nbp = means.shape[1]
    assert nbp >= MOBA_TOP_K
    return pl.pallas_call(
        functools.partial(_moba_select_kernel, heads=heads, ts=ts, nbp=nbp),
        grid=(db,),
        in_specs=[pl.BlockSpec((None, ts, d), lambda b: (b, 0, 0)),
                  pl.BlockSpec((None, nbp, d), lambda b: (b, 0, 0))],
        out_specs=pl.BlockSpec((None, SUBLANES, heads * ts), lambda b: (b, 0, 0)),
        out_shape=jax.ShapeDtypeStruct((db, SUBLANES, heads * ts), jnp.int32),
        compiler_params=pltpu.CompilerParams(
            dimension_semantics=("parallel",), vmem_limit_bytes=VMEM_LIMIT),
        name="moba_select",
    )(q_s, means)


def _sample_moba_kernel(idx_ref, pt_ref, q_ref, knew_ref, vnew_ref, ck_hbm, cv_hbm, o_ref,
                        kbuf, vbuf, sem, *, layer, heads, ts, page, ppb):
    b = pl.program_id(0)
    h = pl.program_id(1)
    g = b * heads + h
    total = pl.num_programs(0) * heads
    slot = g % 2
    ht = heads * ts
    scale = HEAD_DIM ** -0.5

    def block_copies(bb, hh, sl):
        cps = []
        for t in range(ts):
            for k in range(MOBA_TOP_K):
                blk = idx_ref[(bb * SUBLANES + k) * ht + hh * ts + t]
                f = t * MOBA_TOP_K + k
                for u in range(ppb):
                    pg = pt_ref[bb, blk * ppb + u]
                    rows = pl.ds(u * page, page)
                    cps.append(pltpu.make_async_copy(
                        ck_hbm.at[layer, pg, :, hh, :], kbuf.at[sl, f, rows], sem.at[0, sl]))
                    cps.append(pltpu.make_async_copy(
                        cv_hbm.at[layer, pg, :, hh, :], vbuf.at[sl, f, rows], sem.at[1, sl]))
        return cps

    @pl.when(g == 0)
    def _():
        for cp in block_copies(b, h, slot):
            cp.start()

    @pl.when(g + 1 < total)
    def _():
        g1 = g + 1
        for cp in block_copies(g1 // heads, g1 % heads, 1 - slot):
            cp.start()

    for cp in block_copies(b, h, slot):
        cp.wait()

    q = (q_ref[...] * scale).astype(BF16)
    row = lax.broadcasted_iota(jnp.int32, (BF16_ROWS, page), 0)
    key = lax.broadcasted_iota(jnp.int32, (BF16_ROWS, page), 1)
    row_o = lax.broadcasted_iota(jnp.int32, (BF16_ROWS, HEAD_DIM), 0)
    s_own = jnp.where(key <= row, _dot(q, knew_ref[...].astype(BF16), NT_DIMS), NEG)
    m_own = jnp.max(s_own, axis=1, keepdims=True)
    vn = vnew_ref[...].astype(BF16)
    nf = ts * MOBA_TOP_K
    ss = [_dot(q, kbuf[slot, f].astype(BF16), NT_DIMS) for f in range(nf)]
    out = jnp.zeros((BF16_ROWS, HEAD_DIM), F32)
    for t in range(ts):
        fs = [t * MOBA_TOP_K + k for k in range(MOBA_TOP_K)]
        m = m_own
        for f in fs:
            m = jnp.maximum(m, jnp.max(ss[f], axis=1, keepdims=True))
        p = jnp.exp(s_own - m)
        l = jnp.sum(p, axis=1, keepdims=True)
        acc = _dot(p.astype(BF16), vn)
        for f in fs:
            p = jnp.exp(ss[f] - m)
            l = l + jnp.sum(p, axis=1, keepdims=True)
            acc = acc + _dot(p.astype(BF16), vbuf[slot, f].astype(BF16))
        out = jnp.where(row_o == t, acc / l, out)
    o_ref[...] = out


def _sample_moba(idx, page_table, q16, knew, vnew, cache_k, cache_v, layer):
    db, heads = q16.shape[0], q16.shape[1]
    page = cache_k.shape[2]
    ts = idx.shape[0] // (db * SUBLANES * heads)
    ppb = MOBA_BLOCK // page
    nf = ts * MOBA_TOP_K
    per_bh = lambda rows: pl.BlockSpec((None, None, rows, HEAD_DIM), lambda b, h, idx, pt: (b, h, 0, 0))
    return pl.pallas_call(
        functools.partial(_sample_moba_kernel, layer=layer, heads=heads, ts=ts, page=page, ppb=ppb),
        grid_spec=pltpu.PrefetchScalarGridSpec(
            num_scalar_prefetch=2,
            grid=(db, heads),
            in_specs=[per_bh(BF16_ROWS), per_bh(page), per_bh(page),
                      pl.BlockSpec(memory_space=pl.ANY), pl.BlockSpec(memory_space=pl.ANY)],
            out_specs=per_bh(BF16_ROWS),
            scratch_shapes=[pltpu.VMEM((2, nf, MOBA_BLOCK, HEAD_DIM), F32),
                            pltpu.VMEM((2, nf, MOBA_BLOCK, HEAD_DIM), F32),
                            pltpu.SemaphoreType.DMA((2, 2))],
        ),
        out_shape=jax.ShapeDtypeStruct((db, heads, BF16_ROWS, HEAD_DIM), F32),
        compiler_params=pltpu.CompilerParams(
            dimension_semantics=("arbitrary", "arbitrary"), vmem_limit_bytes=VMEM_LIMIT),
        name="sample_attn_moba",
    )(idx, page_table, q16, knew, vnew, cache_k, cache_v)


def _sample_sb_kernel(pt_ref, q_ref, knew_ref, vnew_ref, ck_hbm, cv_hbm, o_ref,
                      kbuf, vbuf, sem, q16_ref, acc_ref, c_ref, flag_ref, *, layer, heads, ts, page,
                      n_pages, pp):
    b = pl.program_id(0)
    s = pl.program_id(1)
    n_steps = n_pages // pp
    g = b * n_steps + s
    total = pl.num_programs(0) * n_steps
    slot = g % 2
    ht = heads * ts
    grp = BF16_ROWS // ts
    scale = HEAD_DIM ** -0.5

    def page_copies(bb, ss, sl):
        cps = []
        for u in range(pp):
            pg = pt_ref[bb, n_pages - 1 - (ss * pp + u)]
            for h in range(heads):
                cps.append(pltpu.make_async_copy(
                    ck_hbm.at[layer, pg, :, h, :], kbuf.at[sl, u, h], sem.at[0, sl]))
                cps.append(pltpu.make_async_copy(
                    cv_hbm.at[layer, pg, :, h, :], vbuf.at[sl, u, h], sem.at[1, sl]))
        return cps

    @pl.when(g == 0)
    def _():
        for cp in page_copies(b, s, slot):
            cp.start()
        flag_ref[1 + slot] = 1

    @pl.when(s == 0)
    def _():
        flag_ref[0] = 0

    fetch_next = (g + 1 < total) & ((s + 1 >= n_steps) | (flag_ref[0] == 0))
    flag_ref[2 - slot] = fetch_next.astype(jnp.int32)

    @pl.when(fetch_next)
    def _():
        g1 = g + 1
        for cp in page_copies(g1 // n_steps, g1 % n_steps, 1 - slot):
            cp.start()

    row_t = lax.broadcasted_iota(jnp.int32, (ht, page), 0) % ts
    lane = lax.broadcasted_iota(jnp.int32, (ht, page), 1)
    grp_row = lax.broadcasted_iota(jnp.int32, (BF16_ROWS, page), 0) // ts
    tri = jnp.where(lax.broadcasted_iota(jnp.int32, (page, page), 0)
                    > lax.broadcasted_iota(jnp.int32, (page, page), 1), 1.0, 0.0).astype(BF16)

    def scores(k_at):
        parts = []
        for gi in range(heads // grp):
            acc = None
            for hh in range(grp):
                h = gi * grp + hh
                z = _dot(q16_ref[h], k_at(h).astype(BF16), NT_DIMS)
                acc = z if acc is None else acc + z
            parts.append(acc)
        return jnp.concatenate(parts, axis=0)

    def weighted_values(a, v_at):
        parts = []
        for gi in range(heads // grp):
            a16 = a[gi * BF16_ROWS:(gi + 1) * BF16_ROWS, :]
            acc = None
            for hh in range(grp):
                h = gi * grp + hh
                w = jnp.where(grp_row == hh, a16, 0.0).astype(BF16)
                o = _dot(w, v_at(h).astype(BF16))
                acc = o if acc is None else acc + o
            parts.append(acc)
        return jnp.concatenate(parts, axis=0)

    def process(pages, c, acc):
        pairs = [_log_sigmoid_pair(scores(k_at)) for k_at, _, _ in pages]
        stays = [jnp.where(m, p[1], 0.0) if m is not None else p[1] for p, (_, _, m) in zip(pairs, pages)]
        splits = [_split_bf16(s) for s in stays]
        afters = [_dot(hi, tri) + _dot(lo, tri) for hi, lo in splits]
        for u, (_, v_at, mask) in enumerate(pages):
            a = jnp.exp(pairs[u][0] + afters[u] + c)
            if mask is not None:
                a = jnp.where(mask, a, 0.0)
            acc = acc + weighted_values(a, v_at)
            c = c + jnp.sum(stays[u], axis=1, keepdims=True)
        return c, acc

    @pl.when(s == 0)
    def _():
        qbd = _block_diag_queries(q_ref[...], heads, ts) * scale
        for h in range(heads):
            gi = h // grp
            q16_ref[h] = qbd[gi * BF16_ROWS:(gi + 1) * BF16_ROWS,
                             h * HEAD_DIM:(h + 1) * HEAD_DIM].astype(BF16)
        new_keys = (lambda h: knew_ref[h], lambda h: vnew_ref[h], lane < row_t)
        c, acc = process([new_keys], jnp.zeros((ht, 1), F32), jnp.zeros((ht, HEAD_DIM), F32))
        c_ref[...] = c
        acc_ref[...] = acc

    have_pages = flag_ref[1 + slot] == 1

    @pl.when(have_pages)
    def _():
        for cp in page_copies(b, s, slot):
            cp.wait()

    @pl.when(have_pages & (flag_ref[0] == 0))
    def _():
        pages = [(lambda h, u=u: kbuf[slot, u, h], lambda h, u=u: vbuf[slot, u, h], None)
                 for u in range(pp)]
        c, acc = process(pages, c_ref[...], acc_ref[...])
        c_ref[...] = c
        acc_ref[...] = acc
        flag_ref[0] = (jnp.max(c) < SB_DEAD).astype(jnp.int32)

    @pl.when(s == n_steps - 1)
    def _():
        o_ref[...] = acc_ref[...]


def _sample_sb(q_s, knew, vnew, cache_k, cache_v, page_table, layer):
    db, ts, d = q_s.shape
    heads = d // HEAD_DIM
    page = cache_k.shape[2]
    n_pages = page_table.shape[1]
    pp = 4 if n_pages % 4 == 0 else 1
    ht = heads * ts
    assert BF16_ROWS % ts == 0 and heads % (BF16_ROWS // ts) == 0
    return pl.pallas_call(
        functools.partial(_sample_sb_kernel, layer=layer, heads=heads, ts=ts, page=page,
                          n_pages=n_pages, pp=pp),
        grid_spec=pltpu.PrefetchScalarGridSpec(
            num_scalar_prefetch=1,
            grid=(db, n_pages // pp),
            in_specs=[pl.BlockSpec((None, ts, d), lambda b, s, pt: (b, 0, 0)),
                      pl.BlockSpec((None, heads, page, HEAD_DIM), lambda b, s, pt: (b, 0, 0, 0)),
                      pl.BlockSpec((None, heads, page, HEAD_DIM), lambda b, s, pt: (b, 0, 0, 0)),
                      pl.BlockSpec(memory_space=pl.ANY), pl.BlockSpec(memory_space=pl.ANY)],
            out_specs=pl.BlockSpec((None, ht, HEAD_DIM), lambda b, s, pt: (b, 0, 0)),
            scratch_shapes=[pltpu.VMEM((2, pp, heads, page, HEAD_DIM), F32),
                            pltpu.VMEM((2, pp, heads, page, HEAD_DIM), F32),
                            pltpu.SemaphoreType.DMA((2, 2)),
                            pltpu.VMEM((heads, BF16_ROWS, HEAD_DIM), BF16),
                            pltpu.VMEM((ht, HEAD_DIM), F32),
                            pltpu.VMEM((ht, 1), F32),
                            pltpu.SMEM((3,), jnp.int32)],
        ),
        out_shape=jax.ShapeDtypeStruct((db, ht, HEAD_DIM), F32),
        compiler_params=pltpu.CompilerParams(
            dimension_semantics=("arbitrary", "arbitrary"), vmem_limit_bytes=VMEM_LIMIT),
        name="sample_attn_sb",
    )(page_table, q_s, knew, vnew, cache_k, cache_v)


def _route(logits, ng, epg):
    lane = lax.broadcasted_iota(jnp.int32, logits.shape, 1).astype(F32)
    ninf = -jnp.inf

    def top(v):
        m = jnp.max(v, axis=1, keepdims=True)
        idx = jnp.min(jnp.where(v == m, lane, float(LANES)), axis=1, keepdims=True)
        return m, idx

    gl = jnp.where(lane < ng, logits, ninf)
    gmax, gidx = top(gl)
    g_p = 1.0 / jnp.sum(jnp.exp(gl - gmax), axis=1, keepdims=True)
    lo = ng + gidx * epg
    el = jnp.where((lane >= lo) & (lane < lo + epg), logits, ninf)
    m1, i1 = top(el)
    m2, i2 = top(jnp.where(lane == i1, ninf, el))
    t = jnp.exp(m2 - m1)
    w0 = g_p / (1.0 + t)
    w1 = g_p * t / (1.0 + t)
    return jnp.where(lane == 0, i1 - ng,
                     jnp.where(lane == 1, i2 - ng,
                               jnp.where(lane == 2, w0, jnp.where(lane == 3, w1, 0.0))))


def _proj_ln_kernel(o_ref, w_ref, x_ref, g_ref, b_ref, wr_ref, br_ref, y_ref, r_ref, *, alpha, ng, epg):
    h = alpha * x_ref[...] + _dot(o_ref[...], w_ref[...])
    y = _layernorm(h, g_ref[...], b_ref[...])
    y_ref[...] = y
    r_ref[...] = _route(_dot3(y, wr_ref[...], NN_DIMS) + br_ref[...], ng, epg)


def _proj_ln(o, w_bf, x, g, b, w_r, b_r, layer, alpha, ng, epg):
    nt, d = x.shape
    tm = ROW_TILE
    return pl.pallas_call(
        functools.partial(_proj_ln_kernel, alpha=alpha, ng=ng, epg=epg),
        grid=(nt // tm,),
        in_specs=[
            pl.BlockSpec((tm, d), lambda i: (i, 0)),
            pl.BlockSpec((None, d, d), lambda i: (layer, 0, 0)),
            pl.BlockSpec((tm, d), lambda i: (i, 0)),
            pl.BlockSpec((None, 1, d), lambda i: (layer, 0, 0)),
            pl.BlockSpec((None, 1, d), lambda i: (layer, 0, 0)),
            pl.BlockSpec((None, d, LANES), lambda i: (layer, 0, 0)),
            pl.BlockSpec((None, 1, LANES), lambda i: (layer, 0, 0)),
        ],
        out_specs=[pl.BlockSpec((tm, d), lambda i: (i, 0)), pl.BlockSpec((tm, LANES), lambda i: (i, 0))],
        out_shape=[jax.ShapeDtypeStruct((nt, d), F32), jax.ShapeDtypeStruct((nt, LANES), F32)],
        compiler_params=pltpu.CompilerParams(
            dimension_semantics=("parallel",), vmem_limit_bytes=VMEM_LIMIT),
        name="proj_ln",
    )(o, w_bf, x, g, b, w_r, b_r)


def _route_plan(e_idx, ne, tm):
    n2 = e_idx.size
    flat = e_idx.reshape(-1)
    onehot = (flat[:, None] == jnp.arange(ne, dtype=jnp.int32)[None, :]).astype(jnp.int32)
    csum = jnp.cumsum(onehot, axis=0)
    counts = csum[-1]
    pos_in_e = jnp.sum((csum - onehot) * onehot, axis=1)
    tiles_e = (counts + tm - 1) // tm
    tile_end = jnp.cumsum(tiles_e)
    tile_start = tile_end - tiles_e
    dest = jnp.sum(onehot * tile_start[None, :], axis=1) * tm + pos_in_e
    n_tiles = -(-n2 // tm) + ne
    gidx = jnp.arange(n_tiles, dtype=jnp.int32)
    active = gidx < tile_end[-1]
    te = jnp.sum((gidx[:, None] >= tile_end[None, :]).astype(jnp.int32), axis=1)
    last_e = jnp.max(jnp.where(counts > 0, jnp.arange(ne, dtype=jnp.int32), 0))
    te = jnp.where(active, te, last_e).astype(jnp.int32)
    tok = jnp.zeros((n_tiles * tm,), jnp.int32).at[dest].set(jnp.arange(n2, dtype=jnp.int32) // 2)
    return dest.astype(jnp.int32), tok, te, active.astype(jnp.int32), n_tiles


def _gather_rows_start(idx_ref, base, n, src_hbm, dst, sem):
    def start(r, carry):
        pltpu.make_async_copy(src_hbm.at[pl.ds(idx_ref[base + r], 1)], dst.at[pl.ds(r, 1)], sem).start()
        return carry

    lax.fori_loop(0, n, start, 0, unroll=8)


def _gather_rows_wait(n, src_hbm, dst, sem):
    def wait(r, carry):
        pltpu.make_async_copy(src_hbm.at[pl.ds(0, 1)], dst.at[pl.ds(r, 1)], sem).wait()
        return carry

    lax.fori_loop(0, n, wait, 0, unroll=8)


def _expert_kernel(te_ref, act_ref, tok_ref, x_hbm, wg_ref, wu_ref, wd_ref, y_ref,
                   xbuf, wgb, wub, wdb, sem, *, tm):
    g = pl.program_id(0)
    n_tiles = pl.num_programs(0)
    slot = g % 2
    nxt = jnp.minimum(g + 1, n_tiles - 1)

    @pl.when((g == 0) & (act_ref[0] == 1))
    def _():
        _gather_rows_start(tok_ref, 0, tm, x_hbm, xbuf.at[0], sem.at[0])

    @pl.when((g + 1 < n_tiles) & (act_ref[nxt] == 1))
    def _():
        _gather_rows_start(tok_ref, nxt * tm, tm, x_hbm, xbuf.at[1 - slot], sem.at[1 - slot])

    prev = te_ref[jnp.maximum(g - 1, 0)]

    @pl.when((g == 0) | (te_ref[g] != prev))
    def _():
        wgb[...] = wg_ref[...].astype(BF16)
        wub[...] = wu_ref[...].astype(BF16)
        wdb[...] = wd_ref[...].astype(BF16)

    @pl.when(act_ref[g] == 1)
    def _():
        _gather_rows_wait(tm, x_hbm, xbuf.at[slot], sem.at[slot])
        xb = xbuf[slot].astype(BF16)
        gate = _dot(xb, wgb[...])
        up = _dot(xb, wub[...])
        h = gate * jax.nn.sigmoid(gate) * up
        y_ref[...] = _dot(h.astype(BF16), wdb[...])

    @pl.when(act_ref[g] == 0)
    def _():
        y_ref[...] = jnp.zeros(y_ref.shape, F32)


def _expert_mlp(x, tok, te, act, n_tiles, w_gate, w_up, w_down, layer, tm):
    nt, d = x.shape
    f = w_gate.shape[-1]
    return pl.pallas_call(
        functools.partial(_expert_kernel, tm=tm),
        grid_spec=pltpu.PrefetchScalarGridSpec(
            num_scalar_prefetch=3,
            grid=(n_tiles,),
            in_specs=[
                pl.BlockSpec(memory_space=pl.ANY),
                pl.BlockSpec((None, None, d, f), lambda g, te, act, tok: (layer, te[g], 0, 0)),
                pl.BlockSpec((None, None, d, f), lambda g, te, act, tok: (layer, te[g], 0, 0)),
                pl.BlockSpec((None, None, f, d), lambda g, te, act, tok: (layer, te[g], 0, 0)),
            ],
            out_specs=pl.BlockSpec((tm, d), lambda g, te, act, tok: (g, 0)),
            scratch_shapes=[
                pltpu.VMEM((2, tm, d), F32),
                pltpu.VMEM((d, f), BF16), pltpu.VMEM((d, f), BF16), pltpu.VMEM((f, d), BF16),
                pltpu.SemaphoreType.DMA((2,)),
            ],
        ),
        out_shape=jax.ShapeDtypeStruct((n_tiles * tm, d), F32),
        compiler_params=pltpu.CompilerParams(
            dimension_semantics=("arbitrary",), vmem_limit_bytes=VMEM_LIMIT),
        name="expert_mlp",
    )(te, act, tok, x, w_gate, w_up, w_down)


def _combine_ln_kernel(dest_ref, r_ref, x_ref, g_ref, b_ref, y_hbm, o_ref, ob_ref, ybuf, sem,
                       *, tm, alpha):
    i = pl.program_id(0)
    n_tiles = pl.num_programs(0)
    slot = i % 2
    rows = tm * TOP_K_EXPERTS

    @pl.when(i == 0)
    def _():
        _gather_rows_start(dest_ref, 0, rows, y_hbm, ybuf.at[0], sem.at[0])

    @pl.when(i + 1 < n_tiles)
    def _():
        _gather_rows_start(dest_ref, (i + 1) * rows, rows, y_hbm, ybuf.at[1 - slot], sem.at[1 - slot])

    _gather_rows_wait(rows, y_hbm, ybuf.at[slot], sem.at[slot])
    y0 = ybuf[slot, 0:tm, :]
    y1 = ybuf[slot, tm:2 * tm, :]
    r = r_ref[...]
    moe = r[:, 2:3] * y0 + r[:, 3:4] * y1
    y = _layernorm(alpha * x_ref[...] + moe, g_ref[...], b_ref[...])
    o_ref[...] = y
    ob_ref[...] = y.astype(BF16)


def _combine_ln(dest, rinfo, x, g, b, y_sorted, layer, alpha):
    nt, d = x.shape
    tm = ROW_TILE
    dest = dest.reshape(nt // tm, tm, TOP_K_EXPERTS).transpose(0, 2, 1).reshape(-1)
    return pl.pallas_call(
        functools.partial(_combine_ln_kernel, tm=tm, alpha=alpha),
        grid_spec=pltpu.PrefetchScalarGridSpec(
            num_scalar_prefetch=1,
            grid=(nt // tm,),
            in_specs=[
                pl.BlockSpec((tm, LANES), lambda i, dest: (i, 0)),
                pl.BlockSpec((tm, d), lambda i, dest: (i, 0)),
                pl.BlockSpec((None, 1, d), lambda i, dest: (layer, 0, 0)),
                pl.BlockSpec((None, 1, d), lambda i, dest: (layer, 0, 0)),
                pl.BlockSpec(memory_space=pl.ANY),
            ],
            out_specs=[pl.BlockSpec((tm, d), lambda i, dest: (i, 0)),
                       pl.BlockSpec((tm, d), lambda i, dest: (i, 0))],
            scratch_shapes=[pltpu.VMEM((2, tm * TOP_K_EXPERTS, d), F32), pltpu.SemaphoreType.DMA((2,))],
        ),
        out_shape=[jax.ShapeDtypeStruct((nt, d), F32), jax.ShapeDtypeStruct((nt, d), BF16)],
        compiler_params=pltpu.CompilerParams(
            dimension_semantics=("arbitrary",), vmem_limit_bytes=VMEM_LIMIT),
        name="combine_ln",
    )(dest, rinfo, x, g, b, y_sorted)


def _rope_tables(pos):
    half = HEAD_DIM // 2
    inv_freq = ROPE_THETA ** (-jnp.arange(half, dtype=F32) / half)
    ang = pos.astype(F32)[:, None] * inv_freq[None, :]
    cos, sin = jnp.cos(ang), jnp.sin(ang)
    return jnp.concatenate([cos, cos], axis=-1), jnp.concatenate([-sin, sin], axis=-1)


def kernel(x_prompt, x_sample, cache_k, cache_v, page_table, w_qkv, w_o, ln_mix_g, ln_mix_b,
           w_group, b_group, w_route, b_route, w_gate, w_up, w_down, ln_ffn_g, ln_ffn_b):
    batch, seq, d = x_prompt.shape
    db, ts, _ = x_sample.shape
    depth = w_qkv.shape[0]
    heads = d // HEAD_DIM
    n_pool, page = cache_k.shape[1], cache_k.shape[2]
    n_pages = page_table.shape[1]
    past = n_pages * page
    ng, ne = w_group.shape[-1], w_route.shape[-1]
    epg = ne // ng
    assert MOBA_BLOCK % page == 0 and past % MOBA_BLOCK == 0 and ts <= page and ts <= BF16_ROWS
    assert ng + ne <= LANES
    alpha = (2 * depth) ** 0.25

    n_p, n_s = batch * seq, db * ts
    nt = n_p + -(-n_s // ROW_TILE) * ROW_TILE
    pad = nt - n_p - n_s

    pos = jnp.concatenate([jnp.tile(jnp.arange(seq, dtype=jnp.int32), batch),
                           jnp.tile(past + jnp.arange(ts, dtype=jnp.int32), db),
                           jnp.zeros((pad,), jnp.int32)])
    cos, sin = _rope_tables(pos)

    x = jnp.concatenate([x_prompt.reshape(n_p, d), x_sample.reshape(n_s, d),
                         jnp.zeros((pad, d), F32)], axis=0)
    xb = x.astype(BF16)
    wo_bf = w_o.astype(BF16)
    w_r = jnp.concatenate([w_group, w_route, jnp.zeros((depth, d, LANES - ng - ne), F32)], axis=-1)
    b_r = jnp.concatenate([b_group, b_route, jnp.zeros((depth, LANES - ng - ne), F32)],
                          axis=-1).reshape(depth, 1, LANES)
    g_mix, b_mix = ln_mix_g.reshape(depth, 1, d), ln_mix_b.reshape(depth, 1, d)
    g_ffn, b_ffn = ln_ffn_g.reshape(depth, 1, d), ln_ffn_b.reshape(depth, 1, d)
    cache_k2 = cache_k.reshape(depth, n_pool, page * heads, HEAD_DIM)

    def per_head(a, rows):
        a = a.reshape(db, ts, heads, HEAD_DIM).transpose(0, 2, 1, 3)
        return jnp.pad(a, ((0, 0), (0, 0), (0, rows - ts), (0, 0)))

    kp = jnp.zeros((depth, n_p, heads, HEAD_DIM), F32)
    vp = jnp.zeros((depth, n_p, heads, HEAD_DIM), F32)
    ks, vs = [], []
    for layer in range(depth):
        kind = layer % 2
        rope = kind == 0
        q_p = _head_proj(xb, w_qkv, cos, sin, layer, 0, rope, n_p)
        kp = _head_proj(xb, w_qkv, cos, sin, layer, 1, rope, n_p, kp)
        vp = _head_proj(xb, w_qkv, cos, sin, layer, 2, False, n_p, vp)
        o_p = _prompt_attn(q_p, kp, vp, layer, kind, batch, seq)
        qkv_s = _qkv_proj(xb[n_p:], w_qkv, cos[n_p:], sin[n_p:], layer, rope=rope)
        q_s, k_s, v_s = (qkv_s[c, :n_s] for c in range(3))
        knew, vnew = per_head(k_s, page), per_head(v_s, page)
        if kind == 0:
            means = _cache_means(cache_k2, page_table, layer, page, heads)
            idx = _moba_select(q_s.reshape(db, ts, d), means.reshape(db, means.shape[1], d), heads)
            o_s = _sample_moba(idx.reshape(-1), page_table, per_head(q_s, BF16_ROWS), knew, vnew,
                               cache_k, cache_v, layer)
            o_s = o_s[:, :, :ts]
        else:
            o_s = _sample_sb(q_s.reshape(db, ts, d), knew, vnew, cache_k, cache_v, page_table, layer)
            o_s = o_s.reshape(db, heads, ts, HEAD_DIM)
        o_s = o_s.transpose(0, 2, 1, 3).reshape(n_s, d)
        o = jnp.concatenate([o_p, o_s.astype(BF16), jnp.zeros((pad, d), BF16)], axis=0)
        x1, rinfo = _proj_ln(o, wo_bf, x, g_mix, b_mix, w_r, b_r, layer, alpha, ng, epg)

        e_idx = rinfo[:, :TOP_K_EXPERTS].astype(jnp.int32)
        dest, tok, te, act, n_tiles = _route_plan(e_idx, ne, ROW_TILE)
        y_sorted = _expert_mlp(x1, tok, te, act, n_tiles, w_gate, w_up, w_down, layer, ROW_TILE)
        x, xb = _combine_ln(dest, rinfo, x1, g_ffn, b_ffn, y_sorted, layer, alpha)

        ks.append(k_s.reshape(db, ts, heads, HEAD_DIM))
        vs.append(v_s.reshape(db, ts, heads, HEAD_DIM))

    return (x[:n_p].reshape(batch, seq, d), x[n_p:n_p + n_s].reshape(db, ts, d),
            kp.reshape(depth, batch, seq, heads, HEAD_DIM), vp.reshape(depth, batch, seq, heads, HEAD_DIM),
            jnp.stack(ks), jnp.stack(vs))
```

```python
import functools

import jax
import jax.numpy as jnp
from jax import lax
from jax.experimental import pallas as pl
from jax.experimental.pallas import tpu as pltpu

HEAD_DIM = 128
MOBA_BLOCK = 256
MOBA_TOP_K = 3
ROPE_THETA = 10000.0
LN_EPS = 1e-5
TOP_K_EXPERTS = 2
LANES = 128
SUBLANES = 8
BF16_ROWS = 16
ROW_TILE = 256
NEG = -1e30
SB_DEAD = -105.0
VMEM_LIMIT = 48 * 1024 * 1024

F32 = jnp.float32
BF16 = jnp.bfloat16
NT_DIMS = (((1,), (1,)), ((), ()))
NN_DIMS = (((1,), (0,)), ((), ()))


def _split_bf16(a):
    hi = a.astype(BF16)
    lo = (a - hi.astype(F32)).astype(BF16)
    return hi, lo


def _dot3(a, b, dims):
    ah, al = _split_bf16(a)
    bh, bl = _split_bf16(b)
    d = lambda x, y: lax.dot_general(x, y, dims, preferred_element_type=F32)
    return d(ah, bh) + d(ah, bl) + d(al, bh)


def _dot(a, b, dims=NN_DIMS):
    return lax.dot_general(a, b, dims, preferred_element_type=F32)


def _pick_tile(n, target, quantum):
    best = quantum
    t = quantum
    while t <= target:
        if n % t == 0:
            best = t
        t += quantum
    assert n % best == 0, (n, quantum)
    return best


def _layernorm(h, g, b):
    mu = jnp.mean(h, axis=-1, keepdims=True)
    d = h - mu
    var = jnp.mean(d * d, axis=-1, keepdims=True)
    return d * lax.rsqrt(var + LN_EPS) * g + b


def _log_sigmoid_pair(z):
    sp = jnp.log(1.0 + jnp.exp(-jnp.abs(z)))
    pos = jnp.minimum(z, 0.0) - sp
    return pos, pos - z


def _block_rank(g, n_blocks, n_valid):
    blk = lax.broadcasted_iota(jnp.int32, g.shape, 0)
    rank = jnp.zeros(g.shape, jnp.int32)
    for n in range(n_blocks):
        gn = g[n:n + 1, :]
        beats = (gn > g) | ((gn == g) & (blk > n))
        inc = jnp.where(beats, 1, 0)
        if n_valid is not None:
            inc = inc * (n < n_valid).astype(jnp.int32)
        rank = rank + inc
    return rank, blk


def _topk_bias(g, n_blocks, n_valid):
    rank, blk = _block_rank(g, n_blocks, n_valid)
    sel = (rank < MOBA_TOP_K) & (blk < n_valid)
    return jnp.where(sel, 0.0, NEG)


def _qkv_kernel(x_ref, w_ref, cos_ref, sin_ref, o_ref, *, rope_tiles):
    acc = _dot(x_ref[...], w_ref[...].astype(BF16))
    if rope_tiles == 0:
        o_ref[...] = acc
        return
    j = pl.program_id(0)

    @pl.when(j < rope_tiles)
    def _():
        cos = cos_ref[...]
        sin = sin_ref[...]
        for c in range(acc.shape[1] // HEAD_DIM):
            blk = acc[:, c * HEAD_DIM:(c + 1) * HEAD_DIM]
            rot = pltpu.roll(blk, HEAD_DIM // 2, axis=1)
            o_ref[:, c * HEAD_DIM:(c + 1) * HEAD_DIM] = blk * cos + rot * sin

    @pl.when(j >= rope_tiles)
    def _():
        o_ref[...] = acc


def _qkv_proj(xb, w_bf, cos, sin, layer, rope):
    nt, d = xb.shape
    tm = _pick_tile(nt, 768, ROW_TILE)
    tn = min(512, d)
    ncol = d // tn
    return pl.pallas_call(
        functools.partial(_qkv_kernel, rope_tiles=2 * ncol if rope else 0),
        grid=(3 * ncol, nt // tm),
        in_specs=[
            pl.BlockSpec((tm, d), lambda j, i: (i, 0)),
            pl.BlockSpec((None, d, tn), lambda j, i: (layer, 0, j)),
            pl.BlockSpec((tm, HEAD_DIM), lambda j, i: (i, 0)),
            pl.BlockSpec((tm, HEAD_DIM), lambda j, i: (i, 0)),
        ],
        out_specs=pl.BlockSpec((None, tm, tn), lambda j, i: (j // ncol, i, j % ncol)),
        out_shape=jax.ShapeDtypeStruct((3, nt, d), F32),
        compiler_params=pltpu.CompilerParams(
            dimension_semantics=("parallel", "parallel"), vmem_limit_bytes=VMEM_LIMIT),
        name="qkv_proj",
    )(xb, w_bf, cos, sin)


def _head_proj_kernel(x_ref, w_ref, cos_ref, sin_ref, *rest, rope, per_head):
    o_ref, wb_ref = rest[-2], rest[-1]

    @pl.when(pl.program_id(1) == 0)
    def _():
        wb_ref[...] = w_ref[...].astype(BF16)

    acc = _dot(x_ref[...], wb_ref[...])
    for c in range(acc.shape[1] // HEAD_DIM):
        blk = acc[:, c * HEAD_DIM:(c + 1) * HEAD_DIM]
        if rope:
            blk = blk * cos_ref[...] + pltpu.roll(blk, HEAD_DIM // 2, axis=1) * sin_ref[...]
        if per_head:
            o_ref[:, c, :] = blk
        else:
            o_ref[:, c * HEAD_DIM:(c + 1) * HEAD_DIM] = blk


def _head_proj(xb, w_bf, cos, sin, layer, part, rope, n_rows, out_buf=None):
    d = xb.shape[1]
    heads = d // HEAD_DIM
    nh = SUBLANES if heads % SUBLANES == 0 else heads
    tn = nh * HEAD_DIM
    ncol = d // tn
    tm = _pick_tile(n_rows, 1024, ROW_TILE)
    in_specs = [
        pl.BlockSpec((tm, d), lambda j, i: (i, 0)),
        pl.BlockSpec((None, d, tn), lambda j, i: (layer, 0, part * ncol + j)),
        pl.BlockSpec((tm, HEAD_DIM), lambda j, i: (i, 0)),
        pl.BlockSpec((tm, HEAD_DIM), lambda j, i: (i, 0)),
    ]
    args = [xb, w_bf, cos, sin]
    if out_buf is None:
        out_specs = pl.BlockSpec((tm, tn), lambda j, i: (i, j))
        out_shape = jax.ShapeDtypeStruct((n_rows, d), F32)
        aliases = {}
    else:
        in_specs.append(pl.BlockSpec(memory_space=pl.ANY))
        args.append(out_buf)
        out_specs = pl.BlockSpec((None, tm, nh, HEAD_DIM), lambda j, i: (layer, i, j, 0))
        out_shape = jax.ShapeDtypeStruct(out_buf.shape, F32)
        aliases = {4: 0}
    return pl.pallas_call(
        functools.partial(_head_proj_kernel, rope=rope, per_head=out_buf is not None),
        grid=(ncol, n_rows // tm),
        in_specs=in_specs,
        out_specs=out_specs,
        out_shape=out_shape,
        input_output_aliases=aliases,
        scratch_shapes=[pltpu.VMEM((d, tn), BF16)],
        compiler_params=pltpu.CompilerParams(
            dimension_semantics=("parallel", "arbitrary"), vmem_limit_bytes=VMEM_LIMIT),
        name="head_proj_" + "qkv"[part],
    )(*args)


def _prompt_attn_kernel(q_ref, k_hbm, v_hbm, o_ref, kst_ref, vst_ref, sem, kb_ref, vt_ref, acc_ref,
                        *scratch, kind, nb, hps, layer):
    blk = MOBA_BLOCK
    seq = nb * blk
    b = pl.program_id(0)
    hg = pl.program_id(1)
    i = pl.program_id(2)
    scale = HEAD_DIM ** -0.5
    cols = lambda hh: slice(hh * HEAD_DIM, (hh + 1) * HEAD_DIM)

    @pl.when(i == 0)
    def _():
        rows = pl.ds(pl.multiple_of(b * seq, blk), seq)
        k_cps = [pltpu.make_async_copy(k_hbm.at[layer, rows, hg * hps + hh, :], kst_ref.at[hh], sem.at[0])
                 for hh in range(hps)]
        v_cps = [pltpu.make_async_copy(v_hbm.at[layer, rows, hg * hps + hh, :], vst_ref.at[hh], sem.at[1])
                 for hh in range(hps)]
        for cp in k_cps + v_cps:
            cp.start()
        if kind == 0:
            scratch[0][...] = jnp.zeros(scratch[0].shape, F32)
        for cp in k_cps:
            cp.wait()
        for hh in range(hps):
            for j in range(nb):
                kj = kst_ref[hh, j * blk:(j + 1) * blk, :]
                kb_ref[hh * nb + j] = kj.astype(BF16)
                if kind == 0:
                    scratch[0][hh, j:j + 1, :] = jnp.sum(kj, axis=0, keepdims=True) * (1.0 / blk)
        for cp in v_cps:
            cp.wait()
        for hh in range(hps):
            for j in range(nb):
                vt_ref[hh * nb + j] = vst_ref[hh, j * blk:(j + 1) * blk, :].T.astype(BF16)

    q = [q_ref[:, cols(hh)] for hh in range(hps)]
    qs = [(x * scale).astype(BF16) for x in q]
    key_i = lax.broadcasted_iota(jnp.int32, (blk, blk), 0)
    qry_i = lax.broadcasted_iota(jnp.int32, (blk, blk), 1)

    def scores(hh, j):
        return _dot(kb_ref[hh * nb + j], qs[hh], NT_DIMS)

    st_ref, done_ref = scratch[-2], scratch[-1]
    heads_r = range(hps)
    if kind == 0:
        means_ref, bias_ref = scratch[0], scratch[1]
        for hh in heads_r:
            gate = _dot3(means_ref[hh], q[hh], NT_DIMS)
            bias_ref[hh] = _topk_bias(gate, nb, i)
        ss = [jnp.where(key_i <= qry_i, scores(hh, i), NEG) for hh in heads_r]
        ms = [jnp.max(s, axis=0, keepdims=True) for s in ss]
        ps = [jnp.exp(s - m) for s, m in zip(ss, ms)]
        pvs = [_dot(vt_ref[hh * nb + i], ps[hh].astype(BF16)) for hh in heads_r]
        for hh in heads_r:
            st_ref[hh, 0:1, :] = ms[hh]
            st_ref[hh, 1:2, :] = jnp.sum(ps[hh], axis=0, keepdims=True)
            acc_ref[hh] = pvs[hh]

        for j in range(nb - 1):
            @pl.when(j < i)
            def _(j=j):
                ss = [scores(hh, j) + bias_ref[hh, j:j + 1, :] for hh in heads_r]
                m_old = [st_ref[hh, 0:1, :] for hh in heads_r]
                m_new = [jnp.maximum(m_old[hh], jnp.max(ss[hh], axis=0, keepdims=True)) for hh in heads_r]
                ps = [jnp.exp(ss[hh] - m_new[hh]) for hh in heads_r]
                pvs = [_dot(vt_ref[hh * nb + j], ps[hh].astype(BF16)) for hh in heads_r]
                for hh in heads_r:
                    a = jnp.exp(m_old[hh] - m_new[hh])
                    st_ref[hh, 0:1, :] = m_new[hh]
                    st_ref[hh, 1:2, :] = a * st_ref[hh, 1:2, :] + jnp.sum(ps[hh], axis=0, keepdims=True)
                    acc_ref[hh] = a * acc_ref[hh] + pvs[hh]

        for hh in heads_r:
            o_ref[:, cols(hh)] = (acc_ref[hh] / st_ref[hh, 1:2, :]).T.astype(o_ref.dtype)
    else:
        tri = jnp.where(qry_i > key_i, 1.0, 0.0).astype(BF16)

        def sb_step(j, cs, mask):
            zs = [scores(hh, j) for hh in heads_r]
            pairs = [_log_sigmoid_pair(z) for z in zs]
            stays = [jnp.where(mask, p[1], 0.0) if mask is not None else p[1] for p in pairs]
            splits = [_split_bf16(s) for s in stays]
            afters = [_dot(tri, hi) + _dot(tri, lo) for hi, lo in splits]
            ws = [jnp.exp(pairs[hh][0] + afters[hh] + cs[hh]) for hh in heads_r]
            if mask is not None:
                ws = [jnp.where(mask, w, 0.0) for w in ws]
            pvs = [_dot(vt_ref[hh * nb + j], ws[hh].astype(BF16)) for hh in heads_r]
            return pvs, [jnp.sum(s, axis=0, keepdims=True) for s in stays]

        pvs, csums = sb_step(i, [0.0] * hps, key_i < qry_i)
        for hh in heads_r:
            st_ref[hh, 0:1, :] = csums[hh]
            acc_ref[hh] = pvs[hh]
        done_ref[0] = 0

        for j in reversed(range(nb - 1)):
            @pl.when((j < i) & (done_ref[0] == 0))
            def _(j=j):
                cs = [st_ref[hh, 0:1, :] for hh in heads_r]
                pvs, csums = sb_step(j, cs, None)
                c_max = None
                for hh in heads_r:
                    acc_ref[hh] = acc_ref[hh] + pvs[hh]
                    c_new = cs[hh] + csums[hh]
                    st_ref[hh, 0:1, :] = c_new
                    c_max = c_new if c_max is None else jnp.maximum(c_max, c_new)
                done_ref[0] = (jnp.max(c_max) < SB_DEAD).astype(jnp.int32)

        for hh in heads_r:
            o_ref[:, cols(hh)] = acc_ref[hh].T.astype(o_ref.dtype)


def _prompt_attn(q, k_buf, v_buf, layer, kind, batch, seq):
    d = q.shape[1]
    heads = d // HEAD_DIM
    blk = MOBA_BLOCK
    assert seq % blk == 0
    nb = seq // blk
    hps = next(n for n in (8, 4, 2, 1) if heads % n == 0)
    w = hps * HEAD_DIM
    scratch = [pltpu.VMEM((hps, seq, HEAD_DIM), F32), pltpu.VMEM((hps, seq, HEAD_DIM), F32),
               pltpu.SemaphoreType.DMA((2,)),
               pltpu.VMEM((hps * nb, blk, HEAD_DIM), BF16), pltpu.VMEM((hps * nb, HEAD_DIM, blk), BF16),
               pltpu.VMEM((hps, HEAD_DIM, blk), F32)]
    if kind == 0:
        nbp = -(-nb // SUBLANES) * SUBLANES
        scratch += [pltpu.VMEM((hps, nbp, HEAD_DIM), F32), pltpu.VMEM((hps, nbp, blk), F32)]
    scratch += [pltpu.VMEM((hps, SUBLANES, blk), F32), pltpu.SMEM((1,), jnp.int32)]
    return pl.pallas_call(
        functools.partial(_prompt_attn_kernel, kind=kind, nb=nb, hps=hps, layer=layer),
        grid=(batch, heads // hps, nb),
        in_specs=[
            pl.BlockSpec((blk, w), lambda b, h, i: (b * nb + i, h)),
            pl.BlockSpec(memory_space=pl.ANY),
            pl.BlockSpec(memory_space=pl.ANY),
        ],
        out_specs=pl.BlockSpec((blk, w), lambda b, h, i: (b * nb + i, h)),
        out_shape=jax.ShapeDtypeStruct((batch * seq, d), BF16),
        scratch_shapes=scratch,
        compiler_params=pltpu.CompilerParams(
            dimension_semantics=("parallel", "parallel", "arbitrary"), vmem_limit_bytes=VMEM_LIMIT),
        name="prompt_attn_moba" if kind == 0 else "prompt_attn_sb",
    )(q, k_buf, v_buf)


def _cache_means_kernel(pt_ref, *refs, bps, ppb, page, heads):
    o_ref = refs[bps * ppb]
    for j in range(bps):
        tot = jnp.zeros((heads, HEAD_DIM), F32)
        for u in range(ppb):
            tot = tot + jnp.sum(refs[j * ppb + u][...].reshape(page, heads, HEAD_DIM), axis=0)
        o_ref[j] = tot * (1.0 / (page * ppb))


def _cache_means(cache2, page_table, layer, page, heads):
    db, n_pages = page_table.shape
    ppb = MOBA_BLOCK // page
    nbp = n_pages // ppb
    bps = 4 if nbp % 4 == 0 else 1
    rows = page * heads

    def page_spec(j, u):
        return pl.BlockSpec((None, None, rows, HEAD_DIM),
                            lambda b, n, pt: (layer, pt[b, (n * bps + j) * ppb + u], 0, 0))

    return pl.pallas_call(
        functools.partial(_cache_means_kernel, bps=bps, ppb=ppb, page=page, heads=heads),
        grid_spec=pltpu.PrefetchScalarGridSpec(
            num_scalar_prefetch=1,
            grid=(db, nbp // bps),
            in_specs=[page_spec(j, u) for j in range(bps) for u in range(ppb)],
            out_specs=pl.BlockSpec((None, bps, heads, HEAD_DIM), lambda b, n, pt: (b, n, 0, 0)),
        ),
        out_shape=jax.ShapeDtypeStruct((db, nbp, heads, HEAD_DIM), F32),
        compiler_params=pltpu.CompilerParams(
            dimension_semantics=("parallel", "parallel"), vmem_limit_bytes=VMEM_LIMIT),
        name="cache_means",
    )(page_table, *([cache2] * (bps * ppb)))


def _block_diag_queries(q, heads, ts):
    rep = jnp.concatenate([q] * heads, axis=0) if heads > 1 else q
    r = lax.broadcasted_iota(jnp.int32, rep.shape, 0) // ts
    c = lax.broadcasted_iota(jnp.int32, rep.shape, 1) // HEAD_DIM
    return jnp.where(r == c, rep, 0.0)


def _moba_select_kernel(q_ref, means_ref, idx_ref, *, heads, ts, nbp):
    qbd = _block_diag_queries(q_ref[...], heads, ts)
    gate = _dot3(means_ref[...], qbd, NT_DIMS)
    rank, blk = _block_rank(gate, nbp, None)
    blk_f = blk.astype(F32)
    rows = [jnp.sum(jnp.where(rank == k, blk_f, 0.0), axis=0, keepdims=True) for k in range(MOBA_TOP_K)]
    rows.append(jnp.zeros((SUBLANES - MOBA_TOP_K, heads * ts), F32))
    idx_ref[...] = jnp.concatenate(rows, axis=0).astype(jnp.int32)


def _moba_select(q_s, means, heads):
    db, ts, d = q_s.shape
    nbp = means.shape[1]
    assert nbp >= MOBA_TOP_K
    return pl.pallas_call(
        functools.partial(_moba_select_kernel, heads=heads, ts=ts, nbp=nbp),
        grid=(db,),
        in_specs=[pl.BlockSpec((None, ts, d), lambda b: (b, 0, 0)),
                  pl.BlockSpec((None, nbp, d), lambda b: (b, 0, 0))],
        out_specs=pl.BlockSpec((None, SUBLANES, heads * ts), lambda b: (b, 0, 0)),
        out_shape=jax.ShapeDtypeStruct((db, SUBLANES, heads * ts), jnp.int32),
        compiler_params=pltpu.CompilerParams(
            dimension_semantics=("parallel",), vmem_limit_bytes=VMEM_LIMIT),
        name="moba_select",
    )(q_s, means)


def _sample_moba_kernel(idx_ref, pt_ref, q_ref, knew_ref, vnew_ref, ck_hbm, cv_hbm, o_ref,
                        kbuf, vbuf, sem, *, layer, heads, ts, page, ppb):
    b = pl.program_id(0)
    h = pl.program_id(1)
    g = b * heads + h
    total = pl.num_programs(0) * heads
    slot = g % 2
    ht = heads * ts
    scale = HEAD_DIM ** -0.5

    def block_copies(bb, hh, sl):
        cps = []
        for t in range(ts):
            for k in range(MOBA_TOP_K):
                blk = idx_ref[(bb * SUBLANES + k) * ht + hh * ts + t]
                f = t * MOBA_TOP_K + k
                for u in range(ppb):
                    pg = pt_ref[bb, blk * ppb + u]
                    rows = pl.ds(u * page, page)
                    cps.append(pltpu.make_async_copy(
                        ck_hbm.at[layer, pg, :, hh, :], kbuf.at[sl, f, rows], sem.at[0, sl]))
                    cps.append(pltpu.make_async_copy(
                        cv_hbm.at[layer, pg, :, hh, :], vbuf.at[sl, f, rows], sem.at[1, sl]))
        return cps

    @pl.when(g == 0)
    def _():
        for cp in block_copies(b, h, slot):
            cp.start()

    @pl.when(g + 1 < total)
    def _():
        g1 = g + 1
        for cp in block_copies(g1 // heads, g1 % heads, 1 - slot):
            cp.start()

    for cp in block_copies(b, h, slot):
        cp.wait()

    q = (q_ref[...] * scale).astype(BF16)
    row = lax.broadcasted_iota(jnp.int32, (BF16_ROWS, page), 0)
    key = lax.broadcasted_iota(jnp.int32, (BF16_ROWS, page), 1)
    row_o = lax.broadcasted_iota(jnp.int32, (BF16_ROWS, HEAD_DIM), 0)
    s_own = jnp.where(key <= row, _dot(q, knew_ref[...].astype(BF16), NT_DIMS), NEG)
    m_own = jnp.max(s_own, axis=1, keepdims=True)
    vn = vnew_ref[...].astype(BF16)
    nf = ts * MOBA_TOP_K
    ss = [_dot(q, kbuf[slot, f].astype(BF16), NT_DIMS) for f in range(nf)]
    out = jnp.zeros((BF16_ROWS, HEAD_DIM), F32)
    for t in range(ts):
        fs = [t * MOBA_TOP_K + k for k in range(MOBA_TOP_K)]
        m = m_own
        for f in fs:
            m = jnp.maximum(m, jnp.max(ss[f], axis=1, keepdims=True))
        p = jnp.exp(s_own - m)
        l = jnp.sum(p, axis=1, keepdims=True)
        acc = _dot(p.astype(BF16), vn)
        for f in fs:
            p = jnp.exp(ss[f] - m)
            l = l + jnp.sum(p, axis=1, keepdims=True)
            acc = acc + _dot(p.astype(BF16), vbuf[slot, f].astype(BF16))
        out = jnp.where(row_o == t, acc / l, out)
    o_ref[...] = out


def _sample_moba(idx, page_table, q16, knew, vnew, cache_k, cache_v, layer):
    db, heads = q16.shape[0], q16.shape[1]
    page = cache_k.shape[2]
    ts = idx.shape[0] // (db * SUBLANES * heads)
    ppb = MOBA_BLOCK // page
    nf = ts * MOBA_TOP_K
    per_bh = lambda rows: pl.BlockSpec((None, None, rows, HEAD_DIM), lambda b, h, idx, pt: (b, h, 0, 0))
    return pl.pallas_call(
        functools.partial(_sample_moba_kernel, layer=layer, heads=heads, ts=ts, page=page, ppb=ppb),
        grid_spec=pltpu.PrefetchScalarGridSpec(
            num_scalar_prefetch=2,
            grid=(db, heads),
            in_specs=[per_bh(BF16_ROWS), per_bh(page), per_bh(page),
                      pl.BlockSpec(memory_space=pl.ANY), pl.BlockSpec(memory_space=pl.ANY)],
            out_specs=per_bh(BF16_ROWS),
            scratch_shapes=[pltpu.VMEM((2, nf, MOBA_BLOCK, HEAD_DIM), F32),
                            pltpu.VMEM((2, nf, MOBA_BLOCK, HEAD_DIM), F32),
                            pltpu.SemaphoreType.DMA((2, 2))],
        ),
        out_shape=jax.ShapeDtypeStruct((db, heads, BF16_ROWS, HEAD_DIM), F32),
        compiler_params=pltpu.CompilerParams(
            dimension_semantics=("arbitrary", "arbitrary"), vmem_limit_bytes=VMEM_LIMIT),
        name="sample_attn_moba",
    )(idx, page_table, q16, knew, vnew, cache_k, cache_v)


def _sample_sb_kernel(pt_ref, q_ref, knew_ref, vnew_ref, ck_hbm, cv_hbm, o_ref,
                      kbuf, vbuf, sem, q16_ref, acc_ref, c_ref, flag_ref, *, layer, heads, ts, page,
                      n_pages, pp):
    b = pl.program_id(0)
    s = pl.program_id(1)
    n_steps = n_pages // pp
    g = b * n_steps + s
    total = pl.num_programs(0) * n_steps
    slot = g % 2
    ht = heads * ts
    grp = BF16_ROWS // ts
    scale = HEAD_DIM ** -0.5

    def page_copies(bb, ss, sl):
        cps = []
        for u in range(pp):
            pg = pt_ref[bb, n_pages - 1 - (ss * pp + u)]
            for h in range(heads):
                cps.append(pltpu.make_async_copy(
                    ck_hbm.at[layer, pg, :, h, :], kbuf.at[sl, u, h], sem.at[0, sl]))
                cps.append(pltpu.make_async_copy(
                    cv_hbm.at[layer, pg, :, h, :], vbuf.at[sl, u, h], sem.at[1, sl]))
        return cps

    @pl.when(g == 0)
    def _():
        for cp in page_copies(b, s, slot):
            cp.start()
        flag_ref[1 + slot] = 1

    @pl.when(s == 0)
    def _():
        flag_ref[0] = 0

    fetch_next = (g + 1 < total) & ((s + 1 >= n_steps) | (flag_ref[0] == 0))
    flag_ref[2 - slot] = fetch_next.astype(jnp.int32)

    @pl.when(fetch_next)
    def _():
        g1 = g + 1
        for cp in page_copies(g1 // n_steps, g1 % n_steps, 1 - slot):
            cp.start()

    row_t = lax.broadcasted_iota(jnp.int32, (ht, page), 0) % ts
    lane = lax.broadcasted_iota(jnp.int32, (ht, page), 1)
    grp_row = lax.broadcasted_iota(jnp.int32, (BF16_ROWS, page), 0) // ts
    tri = jnp.where(lax.broadcasted_iota(jnp.int32, (page, page), 0)
                    > lax.broadcasted_iota(jnp.int32, (page, page), 1), 1.0, 0.0).astype(BF16)

    def scores(k_at):
        parts = []
        for gi in range(heads // grp):
            acc = None
            for hh in range(grp):
                h = gi * grp + hh
                z = _dot(q16_ref[h], k_at(h).astype(BF16), NT_DIMS)
                acc = z if acc is None else acc + z
            parts.append(acc)
        return jnp.concatenate(parts, axis=0)

    def weighted_values(a, v_at):
        parts = []
        for gi in range(heads // grp):
            a16 = a[gi * BF16_ROWS:(gi + 1) * BF16_ROWS, :]
            acc = None
            for hh in range(grp):
                h = gi * grp + hh
                w = jnp.where(grp_row == hh, a16, 0.0).astype(BF16)
                o = _dot(w, v_at(h).astype(BF16))
                acc = o if acc is None else acc + o
            parts.append(acc)
        return jnp.concatenate(parts, axis=0)

    def process(pages, c, acc):
        pairs = [_log_sigmoid_pair(scores(k_at)) for k_at, _, _ in pages]
        stays = [jnp.where(m, p[1], 0.0) if m is not None else p[1] for p, (_, _, m) in zip(pairs, pages)]
        splits = [_split_bf16(s) for s in stays]
        afters = [_dot(hi, tri) + _dot(lo, tri) for hi, lo in splits]
        for u, (_, v_at, mask) in enumerate(pages):
            a = jnp.exp(pairs[u][0] + afters[u] + c)
            if mask is not None:
                a = jnp.where(mask, a, 0.0)
            acc = acc + weighted_values(a, v_at)
            c = c + jnp.sum(stays[u], axis=1, keepdims=True)
        return c, acc

    @pl.when(s == 0)
    def _():
        qbd = _block_diag_queries(q_ref[...], heads, ts) * scale
        for h in range(heads):
            gi = h // grp
            q16_ref[h] = qbd[gi * BF16_ROWS:(gi + 1) * BF16_ROWS,
                             h * HEAD_DIM:(h + 1) * HEAD_DIM].astype(BF16)
        new_keys = (lambda h: knew_ref[h], lambda h: vnew_ref[h], lane < row_t)
        c, acc = process([new_keys], jnp.zeros((ht, 1), F32), jnp.zeros((ht, HEAD_DIM), F32))
        c_ref[...] = c
        acc_ref[...] = acc

    have_pages = flag_ref[1 + slot] == 1

    @pl.when(have_pages)
    def _():
        for cp in page_copies(b, s, slot):
            cp.wait()

    @pl.when(have_pages & (flag_ref[0] == 0))
    def _():
        pages = [(lambda h, u=u: kbuf[slot, u, h], lambda h, u=u: vbuf[slot, u, h], None)
                 for u in range(pp)]
        c, acc = process(pages, c_ref[...], acc_ref[...])
        c_ref[...] = c
        acc_ref[...] = acc
        flag_ref[0] = (jnp.max(c) < SB_DEAD).astype(jnp.int32)

    @pl.when(s == n_steps - 1)
    def _():
        o_ref[...] = acc_ref[...]


def _sample_sb(q_s, knew, vnew, cache_k, cache_v, page_table, layer):
    db, ts, d = q_s.shape
    heads = d // HEAD_DIM
    page = cache_k.shape[2]
    n_pages = page_table.shape[1]
    pp = 4 if n_pages % 4 == 0 else 1
    ht = heads * ts
    assert BF16_ROWS % ts == 0 and heads % (BF16_ROWS // ts) == 0
    return pl.pallas_call(
        functools.partial(_sample_sb_kernel, layer=layer, heads=heads, ts=ts, page=page,
                          n_pages=n_pages, pp=pp),
        grid_spec=pltpu.PrefetchScalarGridSpec(
            num_scalar_prefetch=1,
            grid=(db, n_pages // pp),
            in_specs=[pl.BlockSpec((None, ts, d), lambda b, s, pt: (b, 0, 0)),
                      pl.BlockSpec((None, heads, page, HEAD_DIM), lambda b, s, pt: (b, 0, 0, 0)),
                      pl.BlockSpec((None, heads, page, HEAD_DIM), lambda b, s, pt: (b, 0, 0, 0)),
                      pl.BlockSpec(memory_space=pl.ANY), pl.BlockSpec(memory_space=pl.ANY)],
            out_specs=pl.BlockSpec((None, ht, HEAD_DIM), lambda b, s, pt: (b, 0, 0)),
            scratch_shapes=[pltpu.VMEM((2, pp, heads, page, HEAD_DIM), F32),
                            pltpu.VMEM((2, pp, heads, page, HEAD_DIM), F32),
                            pltpu.SemaphoreType.DMA((2, 2)),
                            pltpu.VMEM((heads, BF16_ROWS, HEAD_DIM), BF16),
                            pltpu.VMEM((ht, HEAD_DIM), F32),
                            pltpu.VMEM((ht, 1), F32),
                            pltpu.SMEM((3,), jnp.int32)],
        ),
        out_shape=jax.ShapeDtypeStruct((db, ht, HEAD_DIM), F32),
        compiler_params=pltpu.CompilerParams(
            dimension_semantics=("arbitrary", "arbitrary"), vmem_limit_bytes=VMEM_LIMIT),
        name="sample_attn_sb",
    )(page_table, q_s, knew, vnew, cache_k, cache_v)


def _route(logits, ng, epg):
    lane = lax.broadcasted_iota(jnp.int32, logits.shape, 1).astype(F32)
    ninf = -jnp.inf

    def top(v):
        m = jnp.max(v, axis=1, keepdims=True)
        idx = jnp.min(jnp.where(v == m, lane, float(LANES)), axis=1, keepdims=True)
        return m, idx

    gl = jnp.where(lane < ng, logits, ninf)
    gmax, gidx = top(gl)
    g_p = 1.0 / jnp.sum(jnp.exp(gl - gmax), axis=1, keepdims=True)
    lo = ng + gidx * epg
    el = jnp.where((lane >= lo) & (lane < lo + epg), logits, ninf)
    m1, i1 = top(el)
    m2, i2 = top(jnp.where(lane == i1, ninf, el))
    t = jnp.exp(m2 - m1)
    w0 = g_p / (1.0 + t)
    w1 = g_p * t / (1.0 + t)
    return jnp.where(lane == 0, i1 - ng,
                     jnp.where(lane == 1, i2 - ng,
                               jnp.where(lane == 2, w0, jnp.where(lane == 3, w1, 0.0))))


def _proj_ln_kernel(o_ref, w_ref, x_ref, g_ref, b_ref, wr_ref, br_ref, y_ref, r_ref, *, alpha, ng, epg):
    h = alpha * x_ref[...] + _dot(o_ref[...], w_ref[...])
    y = _layernorm(h, g_ref[...], b_ref[...])
    y_ref[...] = y
    r_ref[...] = _route(_dot3(y, wr_ref[...], NN_DIMS) + br_ref[...], ng, epg)


def _proj_ln(o, w_bf, x, g, b, w_r, b_r, layer, alpha, ng, epg):
    nt, d = x.shape
    tm = ROW_TILE
    return pl.pallas_call(
        functools.partial(_proj_ln_kernel, alpha=alpha, ng=ng, epg=epg),
        grid=(nt // tm,),
        in_specs=[
            pl.BlockSpec((tm, d), lambda i: (i, 0)),
            pl.BlockSpec((None, d, d), lambda i: (layer, 0, 0)),
            pl.BlockSpec((tm, d), lambda i: (i, 0)),
            pl.BlockSpec((None, 1, d), lambda i: (layer, 0, 0)),
            pl.BlockSpec((None, 1, d), lambda i: (layer, 0, 0)),
            pl.BlockSpec((None, d, LANES), lambda i: (layer, 0, 0)),
            pl.BlockSpec((None, 1, LANES), lambda i: (layer, 0, 0)),
        ],
        out_specs=[pl.BlockSpec((tm, d), lambda i: (i, 0)), pl.BlockSpec((tm, LANES), lambda i: (i, 0))],
        out_shape=[jax.ShapeDtypeStruct((nt, d), F32), jax.ShapeDtypeStruct((nt, LANES), F32)],
        compiler_params=pltpu.CompilerParams(
            dimension_semantics=("parallel",), vmem_limit_bytes=VMEM_LIMIT),
        name="proj_ln",
    )(o, w_bf, x, g, b, w_r, b_r)


def _route_plan(e_idx, ne, tm):
    n2 = e_idx.size
    flat = e_idx.reshape(-1)
    onehot = (flat[:, None] == jnp.arange(ne, dtype=jnp.int32)[None, :]).astype(jnp.int32)
    csum = jnp.cumsum(onehot, axis=0)
    counts = csum[-1]
    pos_in_e = jnp.sum((csum - onehot) * onehot, axis=1)
    tiles_e = (counts + tm - 1) // tm
    tile_end = jnp.cumsum(tiles_e)
    tile_start = tile_end - tiles_e
    dest = jnp.sum(onehot * tile_start[None, :], axis=1) * tm + pos_in_e
    n_tiles = -(-n2 // tm) + ne
    gidx = jnp.arange(n_tiles, dtype=jnp.int32)
    active = gidx < tile_end[-1]
    te = jnp.sum((gidx[:, None] >= tile_end[None, :]).astype(jnp.int32), axis=1)
    last_e = jnp.max(jnp.where(counts > 0, jnp.arange(ne, dtype=jnp.int32), 0))
    te = jnp.where(active, te, last_e).astype(jnp.int32)
    tok = jnp.zeros((n_tiles * tm,), jnp.int32).at[dest].set(jnp.arange(n2, dtype=jnp.int32) // 2)
    first = jnp.concatenate([jnp.ones((1,), jnp.int32), (te[1:] != te[:-1]).astype(jnp.int32)])
    wslot = (jnp.cumsum(first) - 1) % 2
    pos = jnp.where(first == 1, gidx, n_tiles)
    nxt_pos = jnp.concatenate([lax.cummin(pos[::-1])[::-1][1:], jnp.full((1,), n_tiles, jnp.int32)])
    nxt_e = jnp.where(nxt_pos < n_tiles, te[jnp.minimum(nxt_pos, n_tiles - 1)], -1)
    runs = (first, wslot.astype(jnp.int32), nxt_e.astype(jnp.int32))
    return dest.astype(jnp.int32), tok, te, active.astype(jnp.int32), runs, n_tiles


def _gather_rows_start(idx_ref, base, n, src_hbm, dst, sem):
    def start(r, carry):
        pltpu.make_async_copy(src_hbm.at[pl.ds(idx_ref[base + r], 1)], dst.at[pl.ds(r, 1)], sem).start()
        return carry

    lax.fori_loop(0, n, start, 0, unroll=8)


def _gather_rows_wait(n, src_hbm, dst, sem):
    def wait(r, carry):
        pltpu.make_async_copy(src_hbm.at[pl.ds(0, 1)], dst.at[pl.ds(r, 1)], sem).wait()
        return carry

    lax.fori_loop(0, n, wait, 0, unroll=8)


def _expert_kernel(te_ref, act_ref, first_ref, wslot_ref, nxte_ref, tok_ref, x_hbm, wg_hbm, wu_hbm,
                   wd_hbm, y_ref, xbuf, wgf, wuf, wdf, wgb, wub, wdb, sem, wsem, *, tm, layer):
    g = pl.program_id(0)
    n_tiles = pl.num_programs(0)
    slot = g % 2
    nxt = jnp.minimum(g + 1, n_tiles - 1)

    def weight_copies(e, ws):
        return [pltpu.make_async_copy(wg_hbm.at[layer, e], wgf.at[ws], wsem.at[ws]),
                pltpu.make_async_copy(wu_hbm.at[layer, e], wuf.at[ws], wsem.at[ws]),
                pltpu.make_async_copy(wd_hbm.at[layer, e], wdf.at[ws], wsem.at[ws])]

    @pl.when(g == 0)
    def _():
        for cp in weight_copies(te_ref[0], 0):
            cp.start()

    @pl.when((g == 0) & (act_ref[0] == 1))
    def _():
        _gather_rows_start(tok_ref, 0, tm, x_hbm, xbuf.at[0], sem.at[0])

    @pl.when((g + 1 < n_tiles) & (act_ref[nxt] == 1))
    def _():
        _gather_rows_start(tok_ref, nxt * tm, tm, x_hbm, xbuf.at[1 - slot], sem.at[1 - slot])

    @pl.when(first_ref[g] == 1)
    def _():
        ws = wslot_ref[g]

        @pl.when(nxte_ref[g] >= 0)
        def _():
            for cp in weight_copies(nxte_ref[g], 1 - ws):
                cp.start()

        for cp in weight_copies(te_ref[g], ws):
            cp.wait()
        wgb[...] = wgf[ws].astype(BF16)
        wub[...] = wuf[ws].astype(BF16)
        wdb[...] = wdf[ws].astype(BF16)

    @pl.when(act_ref[g] == 1)
    def _():
        _gather_rows_wait(tm, x_hbm, xbuf.at[slot], sem.at[slot])
        xb = xbuf[slot].astype(BF16)
        gate = _dot(xb, wgb[...])
        up = _dot(xb, wub[...])
        h = gate * jax.nn.sigmoid(gate) * up
        y_ref[...] = _dot(h.astype(BF16), wdb[...])

    @pl.when(act_ref[g] == 0)
    def _():
        y_ref[...] = jnp.zeros(y_ref.shape, F32)


def _expert_mlp(x, tok, te, act, runs, n_tiles, w_gate, w_up, w_down, layer, tm):
    nt, d = x.shape
    f = w_gate.shape[-1]
    any_spec = pl.BlockSpec(memory_space=pl.ANY)
    return pl.pallas_call(
        functools.partial(_expert_kernel, tm=tm, layer=layer),
        grid_spec=pltpu.PrefetchScalarGridSpec(
            num_scalar_prefetch=6,
            grid=(n_tiles,),
            in_specs=[any_spec, any_spec, any_spec, any_spec],
            out_specs=pl.BlockSpec((tm, d), lambda g, *_: (g, 0)),
            scratch_shapes=[
                pltpu.VMEM((2, tm, d), F32),
                pltpu.VMEM((2, d, f), F32), pltpu.VMEM((2, d, f), F32), pltpu.VMEM((2, f, d), F32),
                pltpu.VMEM((d, f), BF16), pltpu.VMEM((d, f), BF16), pltpu.VMEM((f, d), BF16),
                pltpu.SemaphoreType.DMA((2,)), pltpu.SemaphoreType.DMA((2,)),
            ],
        ),
        out_shape=jax.ShapeDtypeStruct((n_tiles * tm, d), F32),
        compiler_params=pltpu.CompilerParams(
            dimension_semantics=("arbitrary",), vmem_limit_bytes=VMEM_LIMIT),
        name="expert_mlp",
    )(te, act, *runs, tok, x, w_gate, w_up, w_down)


def _combine_ln_kernel(dest_ref, r_ref, x_ref, g_ref, b_ref, y_hbm, o_ref, ob_ref, ybuf, sem,
                       *, tm, alpha):
    i = pl.program_id(0)
    n_tiles = pl.num_programs(0)
    slot = i % 2
    rows = tm * TOP_K_EXPERTS

    @pl.when(i == 0)
    def _():
        _gather_rows_start(dest_ref, 0, rows, y_hbm, ybuf.at[0], sem.at[0])

    @pl.when(i + 1 < n_tiles)
    def _():
        _gather_rows_start(dest_ref, (i + 1) * rows, rows, y_hbm, ybuf.at[1 - slot], sem.at[1 - slot])

    _gather_rows_wait(rows, y_hbm, ybuf.at[slot], sem.at[slot])
    y0 = ybuf[slot, 0:tm, :]
    y1 = ybuf[slot, tm:2 * tm, :]
    r = r_ref[...]
    moe = r[:, 2:3] * y0 + r[:, 3:4] * y1
    y = _layernorm(alpha * x_ref[...] + moe, g_ref[...], b_ref[...])
    o_ref[...] = y
    ob_ref[...] = y.astype(BF16)


def _combine_ln(dest, rinfo, x, g, b, y_sorted, layer, alpha):
    nt, d = x.shape
    tm = ROW_TILE
    dest = dest.reshape(nt // tm, tm, TOP_K_EXPERTS).transpose(0, 2, 1).reshape(-1)
    return pl.pallas_call(
        functools.partial(_combine_ln_kernel, tm=tm, alpha=alpha),
        grid_spec=pltpu.PrefetchScalarGridSpec(
            num_scalar_prefetch=1,
            grid=(nt // tm,),
            in_specs=[
                pl.BlockSpec((tm, LANES), lambda i, dest: (i, 0)),
                pl.BlockSpec((tm, d), lambda i, dest: (i, 0)),
                pl.BlockSpec((None, 1, d), lambda i, dest: (layer, 0, 0)),
                pl.BlockSpec((None, 1, d), lambda i, dest: (layer, 0, 0)),
                pl.BlockSpec(memory_space=pl.ANY),
            ],
            out_specs=[pl.BlockSpec((tm, d), lambda i, dest: (i, 0)),
                       pl.BlockSpec((tm, d), lambda i, dest: (i, 0))],
            scratch_shapes=[pltpu.VMEM((2, tm * TOP_K_EXPERTS, d), F32), pltpu.SemaphoreType.DMA((2,))],
        ),
        out_shape=[jax.ShapeDtypeStruct((nt, d), F32), jax.ShapeDtypeStruct((nt, d), BF16)],
        compiler_params=pltpu.CompilerParams(
            dimension_semantics=("arbitrary",), vmem_limit_bytes=VMEM_LIMIT),
        name="combine_ln",
    )(dest, rinfo, x, g, b, y_sorted)


def _rope_tables(pos):
    half = HEAD_DIM // 2
    inv_freq = ROPE_THETA ** (-jnp.arange(half, dtype=F32) / half)
    ang = pos.astype(F32)[:, None] * inv_freq[None, :]
    cos, sin = jnp.cos(ang), jnp.sin(ang)
    return jnp.concatenate([cos, cos], axis=-1), jnp.concatenate([-sin, sin], axis=-1)


def kernel(x_prompt, x_sample, cache_k, cache_v, page_table, w_qkv, w_o, ln_mix_g, ln_mix_b,
           w_group, b_group, w_route, b_route, w_gate, w_up, w_down, ln_ffn_g, ln_ffn_b):
    batch, seq, d = x_prompt.shape
    db, ts, _ = x_sample.shape
    depth = w_qkv.shape[0]
    heads = d // HEAD_DIM
    n_pool, page = cache_k.shape[1], cache_k.shape[2]
    n_pages = page_table.shape[1]
    past = n_pages * page
    ng, ne = w_group.shape[-1], w_route.shape[-1]
    epg = ne // ng
    assert MOBA_BLOCK % page == 0 and past % MOBA_BLOCK == 0 and ts <= page and ts <= BF16_ROWS
    assert ng + ne <= LANES
    alpha = (2 * depth) ** 0.25

    n_p, n_s = batch * seq, db * ts
    nt = n_p + -(-n_s // ROW_TILE) * ROW_TILE
    pad = nt - n_p - n_s

    pos = jnp.concatenate([jnp.tile(jnp.arange(seq, dtype=jnp.int32), batch),
                           jnp.tile(past + jnp.arange(ts, dtype=jnp.int32), db),
                           jnp.zeros((pad,), jnp.int32)])
    cos, sin = _rope_tables(pos)

    x = jnp.concatenate([x_prompt.reshape(n_p, d), x_sample.reshape(n_s, d),
                         jnp.zeros((pad, d), F32)], axis=0)
    xb = x.astype(BF16)
    wo_bf = w_o.astype(BF16)
    w_r = jnp.concatenate([w_group, w_route, jnp.zeros((depth, d, LANES - ng - ne), F32)], axis=-1)
    b_r = jnp.concatenate([b_group, b_route, jnp.zeros((depth, LANES - ng - ne), F32)],
                          axis=-1).reshape(depth, 1, LANES)
    g_mix, b_mix = ln_mix_g.reshape(depth, 1, d), ln_mix_b.reshape(depth, 1, d)
    g_ffn, b_ffn = ln_ffn_g.reshape(depth, 1, d), ln_ffn_b.reshape(depth, 1, d)
    cache_k2 = cache_k.reshape(depth, n_pool, page * heads, HEAD_DIM)

    def per_head(a, rows):
        a = a.reshape(db, ts, heads, HEAD_DIM).transpose(0, 2, 1, 3)
        return jnp.pad(a, ((0, 0), (0, 0), (0, rows - ts), (0, 0)))

    kp = jnp.zeros((depth, n_p, heads, HEAD_DIM), F32)
    vp = jnp.zeros((depth, n_p, heads, HEAD_DIM), F32)
    ks, vs = [], []
    for layer in range(depth):
        kind = layer % 2
        rope = kind == 0
        q_p = _head_proj(xb, w_qkv, cos, sin, layer, 0, rope, n_p)
        kp = _head_proj(xb, w_qkv, cos, sin, layer, 1, rope, n_p, kp)
        vp = _head_proj(xb, w_qkv, cos, sin, layer, 2, False, n_p, vp)
        o_p = _prompt_attn(q_p, kp, vp, layer, kind, batch, seq)
        qkv_s = _qkv_proj(xb[n_p:], w_qkv, cos[n_p:], sin[n_p:], layer, rope=rope)
        q_s, k_s, v_s = (qkv_s[c, :n_s] for c in range(3))
        knew, vnew = per_head(k_s, page), per_head(v_s, page)
        if kind == 0:
            means = _cache_means(cache_k2, page_table, layer, page, heads)
            idx = _moba_select(q_s.reshape(db, ts, d), means.reshape(db, means.shape[1], d), heads)
            o_s = _sample_moba(idx.reshape(-1), page_table, per_head(q_s, BF16_ROWS), knew, vnew,
                               cache_k, cache_v, layer)
            o_s = o_s[:, :, :ts]
        else:
            o_s = _sample_sb(q_s.reshape(db, ts, d), knew, vnew, cache_k, cache_v, page_table, layer)
            o_s = o_s.reshape(db, heads, ts, HEAD_DIM)
        o_s = o_s.transpose(0, 2, 1, 3).reshape(n_s, d)
        o = jnp.concatenate([o_p, o_s.astype(BF16), jnp.zeros((pad, d), BF16)], axis=0)
        x1, rinfo = _proj_ln(o, wo_bf, x, g_mix, b_mix, w_r, b_r, layer, alpha, ng, epg)

        e_idx = rinfo[:, :TOP_K_EXPERTS].astype(jnp.int32)
        dest, tok, te, act, runs, n_tiles = _route_plan(e_idx, ne, ROW_TILE)
        y_sorted = _expert_mlp(x1, tok, te, act, runs, n_tiles, w_gate, w_up, w_down, layer, ROW_TILE)
        x, xb = _combine_ln(dest, rinfo, x1, g_ffn, b_ffn, y_sorted, layer, alpha)

        ks.append(k_s.reshape(db, ts, heads, HEAD_DIM))
        vs.append(v_s.reshape(db, ts, heads, HEAD_DIM))

    return (x[:n_p].reshape(batch, seq, d), x[n_p:n_p + n_s].reshape(db, ts, d),
            kp.reshape(depth, batch, seq, heads, HEAD_DIM), vp.reshape(depth, batch, seq, heads, HEAD_DIM),
            jnp.stack(ks), jnp.stack(vs))
```
